```python
import math
import jax, jax.numpy as jnp
from jax import lax
import numpy as np

D_MODEL = 1024
BATCH = 16
SEQ = 2048
DEPTH = 4
DEC_BATCH = 32
DEC_SEQ = 64
PAST_LEN = 1024

CHUNK = 64
N_A_LAYERS = DEPTH // 2
N_B_LAYERS = DEPTH - N_A_LAYERS
SSM_EXPAND = 2
D_INNER = SSM_EXPAND * D_MODEL
SSM_HEAD_DIM = 64
SSM_HEADS = D_INNER // SSM_HEAD_DIM
SSM_GROUPS = 4
HEADS_PER_GROUP = SSM_HEADS // SSM_GROUPS
SSM_STATE = 128
D_CONV = 4
CONV_DIM = D_INNER + 2 * SSM_GROUPS * SSM_STATE
IN_DIM = D_INNER + CONV_DIM + SSM_HEADS
SSD_CHUNK = CHUNK
RMS_EPS = 1e-5
ATT_HEAD_DIM = 64
ATT_HEADS = D_MODEL // ATT_HEAD_DIM
ATT_WIDTH = ATT_HEADS * ATT_HEAD_DIM
ATT_SCALE = ATT_HEAD_DIM ** -0.5
Q_BLOCK = 128
FORGET_BIAS_INIT = 3.0
PEER_HEADS = 8
PEER_NKEYS = 128
PEER_EXPERTS = PEER_NKEYS * PEER_NKEYS
PEER_QDIM = 256
PEER_HALF = PEER_QDIM // 2
PEER_TOPK = 16
PEER_TOK_BLOCK = 256
DN_ALPHA = (2.0 * DEPTH) ** 0.25
DN_BETA = (8.0 * DEPTH) ** -0.25
LN_EPS = 1e-5

kernel_name = "yoco_mamba2_fox_peer_stream_step"


def layer_norm(x, g, b):
    xf = x.astype(jnp.float32)
    mu = jnp.mean(xf, -1, keepdims=True)
    var = jnp.mean(jnp.square(xf - mu), -1, keepdims=True)
    return ((xf - mu) * lax.rsqrt(var + LN_EPS) * g.astype(jnp.float32) + b.astype(jnp.float32)).astype(x.dtype)


def gated_group_rms_norm(y, z, w):
    shp = y.shape
    h = (y * jax.nn.silu(z)).astype(jnp.float32).reshape(shp[:-1] + (SSM_GROUPS, D_INNER // SSM_GROUPS))
    h = h * lax.rsqrt(jnp.mean(h * h, -1, keepdims=True) + RMS_EPS)
    return (h.reshape(shp) * w.astype(jnp.float32)).astype(y.dtype)


def causal_dwconv(u, conv_state, w, b):
    L = u.shape[1]
    ext = jnp.concatenate([conv_state.astype(u.dtype), u], axis=1)
    out = b.astype(u.dtype) + ext[:, 0:L] * w[0]
    for k in range(1, D_CONV):
        out = out + ext[:, k:k + L] * w[k]
    return out, ext[:, L:]


def segsum(a):
    T = a.shape[-1]
    cs = jnp.cumsum(a, -1)
    diff = cs[..., :, None] - cs[..., None, :]
    mask = jnp.tril(jnp.ones((T, T), dtype=bool))
    return jnp.where(mask, diff, -jnp.inf)


def ssd_scan(x, a, Bm, Cm, init_state):
    b, L, G, R, P = x.shape
    N = Bm.shape[-1]
    Q = min(SSD_CHUNK, L)
    nc = L // Q
    xc = x.reshape(b, nc, Q, G, R, P)
    Bc = Bm.reshape(b, nc, Q, G, N)
    Cc = Cm.reshape(b, nc, Q, G, N)
    ac = a.reshape(b, nc, Q, G, R).transpose(0, 3, 4, 1, 2)
    a_cs = jnp.cumsum(ac, -1)
    Lmat = jnp.exp(segsum(ac))
    CB = jnp.einsum("bclgn,bcsgn->bcgls", Cc, Bc)
    y_diag = jnp.einsum("bcgls,bgrcls,bcsgrp->bclgrp", CB, Lmat, xc)
    decay_states = jnp.exp(a_cs[..., -1:] - a_cs)
    chunk_states = jnp.einsum("bclgn,bgrcl,bclgrp->bcgrpn", Bc, decay_states, xc)
    chunk_decay = jnp.exp(a_cs[..., -1])

    def step(h, inp):
        s_c, d_c = inp
        return h * d_c[..., None, None] + s_c, h

    final, states_in = lax.scan(step, init_state,
                                (jnp.moveaxis(chunk_states, 1, 0), jnp.moveaxis(chunk_decay, 3, 0)))
    states_in = jnp.moveaxis(states_in, 0, 1)
    y_off = jnp.einsum("bclgn,bcgrpn,bgrcl->bclgrp", Cc, states_in, jnp.exp(a_cs))
    return (y_diag + y_off).reshape(b, L, G, R, P), final


def mamba2_mixer(x, conv_state, ssm_state, w_in, conv_w, conv_b, dt_bias, A_log, D_skip, norm_w, w_out):
    f32 = jnp.float32
    b, L, _ = x.shape
    z, xBC, dt = jnp.split(x @ w_in, [D_INNER, D_INNER + CONV_DIM], axis=-1)
    xBC, new_conv = causal_dwconv(xBC, conv_state, conv_w, conv_b)
    xBC = jax.nn.silu(xBC)
    xs, Bm, Cm = jnp.split(xBC, [D_INNER, D_INNER + SSM_GROUPS * SSM_STATE], axis=-1)
    xs = xs.reshape(b, L, SSM_GROUPS, HEADS_PER_GROUP, SSM_HEAD_DIM).astype(f32)
    Bm = Bm.reshape(b, L, SSM_GROUPS, SSM_STATE).astype(f32)
    Cm = Cm.reshape(b, L, SSM_GROUPS, SSM_STATE).astype(f32)
    dt = jax.nn.softplus(dt.astype(f32) + dt_bias.astype(f32)).reshape(b, L, SSM_GROUPS, HEADS_PER_GROUP)
    A = -jnp.exp(A_log.astype(f32)).reshape(SSM_GROUPS, HEADS_PER_GROUP)
    h0 = ssm_state.astype(f32).reshape(b, SSM_GROUPS, HEADS_PER_GROUP, SSM_HEAD_DIM, SSM_STATE)
    y, h_final = ssd_scan(xs * dt[..., None], dt * A, Bm, Cm, h0)
    y = y + xs * D_skip.astype(f32).reshape(SSM_GROUPS, HEADS_PER_GROUP)[:, :, None]
    y = gated_group_rms_norm(y.reshape(b, L, D_INNER).astype(x.dtype), z, norm_w)
    return y @ w_out, new_conv, h_final.reshape(b, SSM_HEADS, SSM_HEAD_DIM, SSM_STATE)


def shared_kv(h, kv_w, kv_b_f):
    b, L, _ = h.shape
    k, v, f = jnp.split(h @ kv_w, [ATT_WIDTH, 2 * ATT_WIDTH], axis=-1)
    logf = jax.nn.log_sigmoid(f.astype(jnp.float32) + kv_b_f.astype(jnp.float32))
    return (k.reshape(b, L, ATT_HEADS, ATT_HEAD_DIM), v.reshape(b, L, ATT_HEADS, ATT_HEAD_DIM), logf)


def forgetting_attention(q, cq, qpos, k, v, ck, kpos):
    s = jnp.einsum("bqhd,bkhd->bhqk", q, k).astype(jnp.float32) * ATT_SCALE
    s = s + jnp.transpose(cq, (0, 2, 1))[..., :, None] - jnp.transpose(ck, (0, 2, 1))[..., None, :]
    mask = kpos[None, :] <= qpos[:, None]
    p = jax.nn.softmax(jnp.where(mask, s, -jnp.inf), axis=-1).astype(v.dtype)
    return jnp.einsum("bhqk,bkhd->bqhd", p, v)


def fox_mixer(x, k, v, ck, w_qg, w_o):
    b, L, _ = x.shape
    Lk = k.shape[1]
    q, g = jnp.split(x @ w_qg, 2, axis=-1)
    q = q.reshape(b, L, ATT_HEADS, ATT_HEAD_DIM)
    kpos = jnp.arange(Lk)
    qpos = (Lk - L) + jnp.arange(L)
    cq = ck[:, Lk - L:]
    if L > Q_BLOCK:
        nb = L // Q_BLOCK
        qb = q.reshape(b, nb, Q_BLOCK, ATT_HEADS, ATT_HEAD_DIM).swapaxes(0, 1)
        cqb = cq.reshape(b, nb, Q_BLOCK, ATT_HEADS).swapaxes(0, 1)
        pb = qpos.reshape(nb, Q_BLOCK)
        o = lax.map(lambda a: forgetting_attention(a[0], a[1], a[2], k, v, ck, kpos), (qb, cqb, pb))
        o = o.swapaxes(0, 1).reshape(b, L, ATT_HEADS, ATT_HEAD_DIM)
    else:
        o = forgetting_attention(q, cq, qpos, k, v, ck, kpos)
    o = o.reshape(b, L, ATT_WIDTH) * jax.nn.sigmoid(g)
    return o @ w_o


def peer_ffn(x, w_q, subkeys, u_tab, v_tab):
    shp = x.shape
    xt = x.reshape(-1, D_MODEL)
    T = xt.shape[0]
    q = (xt @ w_q).reshape(T, PEER_HEADS, 2, PEER_HALF)
    s = jnp.einsum("thkd,knd->thkn", q, subkeys).astype(jnp.float32)
    sv, si = lax.top_k(s, PEER_TOPK)
    cand = sv[..., 0, :, None] + sv[..., 1, None, :]
    best, bi = lax.top_k(cand.reshape(T, PEER_HEADS, PEER_TOPK * PEER_TOPK), PEER_TOPK)
    i1 = jnp.take_along_axis(si[..., 0, :], bi // PEER_TOPK, axis=-1)
    i2 = jnp.take_along_axis(si[..., 1, :], bi % PEER_TOPK, axis=-1)
    experts = (i1 * PEER_NKEYS + i2).reshape(T, PEER_HEADS * PEER_TOPK)
    gates = jax.nn.softmax(best, axis=-1).reshape(T, PEER_HEADS * PEER_TOPK)
    nblk = -(-T // PEER_TOK_BLOCK)
    pad = nblk * PEER_TOK_BLOCK - T
    hk = PEER_HEADS * PEER_TOPK
    xp = jnp.pad(xt, ((0, pad), (0, 0))).reshape(nblk, PEER_TOK_BLOCK, D_MODEL)
    ep = jnp.pad(experts, ((0, pad), (0, 0))).reshape(nblk, PEER_TOK_BLOCK, hk)
    gp = jnp.pad(gates, ((0, pad), (0, 0))).reshape(nblk, PEER_TOK_BLOCK, hk)

    def block(args):
        xb, eb, gb = args
        u = jnp.take(u_tab, eb, axis=0)
        act = jax.nn.gelu(jnp.einsum("tkd,td->tk", u, xb).astype(jnp.float32), approximate=False) * gb
        vv = jnp.take(v_tab, eb, axis=0)
        return jnp.einsum("tk,tkd->td", act.astype(xb.dtype), vv)

    out = lax.map(block, (xp, ep, gp)).reshape(nblk * PEER_TOK_BLOCK, D_MODEL)[:T]
    return out.reshape(shp)


def trunk(x, conv_state, ssm_state, past_k, past_v, past_logf,
          a_w_in, a_conv_w, a_conv_b, a_dt_bias, a_A_log, a_D, a_norm_w, a_w_out,
          kv_w, kv_b_f, b_w_qg, b_w_o, peer_w_q, peer_subkeys, peer_u, peer_v, ln_g, ln_b):
    new_conv, new_ssm = [], []
    for i in range(DEPTH):
        if i < N_A_LAYERS:
            y, c_i, s_i = mamba2_mixer(x, conv_state[i], ssm_state[i], a_w_in[i], a_conv_w[i], a_conv_b[i],
                                       a_dt_bias[i], a_A_log[i], a_D[i], a_norm_w[i], a_w_out[i])
            new_conv.append(c_i)
            new_ssm.append(s_i)
        else:
            j = i - N_A_LAYERS
            y = fox_mixer(x, k_all, v_all, ck, b_w_qg[j], b_w_o[j])
        x = layer_norm(DN_ALPHA * x + y, ln_g[i, 0], ln_b[i, 0])
        x = layer_norm(DN_ALPHA * x + peer_ffn(x, peer_w_q[i], peer_subkeys[i], peer_u[i], peer_v[i]),
                       ln_g[i, 1], ln_b[i, 1])
        if i == N_A_LAYERS - 1:
            k_new, v_new, logf_new = shared_kv(x, kv_w, kv_b_f)
            k_all = jnp.concatenate([past_k.astype(k_new.dtype), k_new], axis=1)
            v_all = jnp.concatenate([past_v.astype(v_new.dtype), v_new], axis=1)
            ck = jnp.cumsum(jnp.concatenate([past_logf.astype(jnp.float32), logf_new], axis=1), axis=1)
    return x, jnp.stack(new_conv), jnp.stack(new_ssm), k_new, v_new, logf_new


def setup_inputs(seed: int = 0) -> dict:
    key = jax.random.key(seed)
    ks = jax.random.split(key, 32)
    f32 = jnp.float32

    def nrm(k, shape, scale):
        return jax.random.normal(k, shape, f32) * scale

    dt0 = jnp.exp(jax.random.uniform(ks[10], (N_A_LAYERS, SSM_HEADS), f32, math.log(1e-3), math.log(1e-1)))
    kv_w = nrm(ks[17], (D_MODEL, 2 * ATT_WIDTH + ATT_HEADS), D_MODEL ** -0.5)
    kv_w = kv_w.at[:, ATT_WIDTH:2 * ATT_WIDTH].multiply(DN_BETA)
    return {
        "x_prompt": nrm(ks[0], (BATCH, SEQ, D_MODEL), 1.0),
        "x_sample": nrm(ks[1], (DEC_BATCH, DEC_SEQ, D_MODEL), 1.0),
        "state_ssm": nrm(ks[2], (N_A_LAYERS, DEC_BATCH, SSM_HEADS, SSM_HEAD_DIM, SSM_STATE), 0.1),
        "state_conv": nrm(ks[3], (N_A_LAYERS, DEC_BATCH, D_CONV - 1, CONV_DIM), 1.0),
        "cache_k": nrm(ks[4], (DEC_BATCH, PAST_LEN, ATT_HEADS, ATT_HEAD_DIM), 1.0),
        "cache_v": nrm(ks[5], (DEC_BATCH, PAST_LEN, ATT_HEADS, ATT_HEAD_DIM), DN_BETA),
        "cache_logf": jax.nn.log_sigmoid(FORGET_BIAS_INIT + nrm(ks[6], (DEC_BATCH, PAST_LEN, ATT_HEADS), 1.0)),
        "a_w_in": nrm(ks[7], (N_A_LAYERS, D_MODEL, IN_DIM), D_MODEL ** -0.5),
        "a_conv_w": nrm(ks[8], (N_A_LAYERS, D_CONV, CONV_DIM), D_CONV ** -0.5),
        "a_conv_b": nrm(ks[9], (N_A_LAYERS, CONV_DIM), 0.02),
        "a_dt_bias": dt0 + jnp.log(-jnp.expm1(-dt0)),
        "a_A_log": jnp.log(jax.random.uniform(ks[11], (N_A_LAYERS, SSM_HEADS), f32, 1.0, 16.0)),
        "a_D": 1.0 + nrm(ks[12], (N_A_LAYERS, SSM_HEADS), 0.1),
        "a_norm_w": 1.0 + nrm(ks[13], (N_A_LAYERS, D_INNER), 0.02),
        "a_w_out": nrm(ks[14], (N_A_LAYERS, D_INNER, D_MODEL), DN_BETA * D_INNER ** -0.5),
        "kv_w": kv_w,
        "kv_b_f": FORGET_BIAS_INIT + nrm(ks[15], (ATT_HEADS,), 0.1),
        "b_w_qg": nrm(ks[16], (N_B_LAYERS, D_MODEL, 2 * ATT_WIDTH), D_MODEL ** -0.5),
        "b_w_o": nrm(ks[18], (N_B_LAYERS, ATT_WIDTH, D_MODEL), DN_BETA * ATT_WIDTH ** -0.5),
        "peer_w_q": nrm(ks[19], (DEPTH, D_MODEL, PEER_HEADS * PEER_QDIM), D_MODEL ** -0.5),
        "peer_subkeys": nrm(ks[20], (DEPTH, 2, PEER_NKEYS, PEER_HALF), PEER_HALF ** -0.5),
        "peer_u": nrm(ks[21], (DEPTH, PEER_EXPERTS, D_MODEL), D_MODEL ** -0.5),
        "peer_v": nrm(ks[22], (DEPTH, PEER_EXPERTS, D_MODEL), DN_BETA * PEER_HEADS ** -0.5),
        "ln_g": 1.0 + nrm(ks[23], (DEPTH, 2, D_MODEL), 0.02),
        "ln_b": nrm(ks[24], (DEPTH, 2, D_MODEL), 0.02),
    }


def reference(x_prompt, x_sample, state_ssm, state_conv, cache_k, cache_v, cache_logf,
              a_w_in, a_conv_w, a_conv_b, a_dt_bias, a_A_log, a_D, a_norm_w, a_w_out,
              kv_w, kv_b_f, b_w_qg, b_w_o, peer_w_q, peer_subkeys, peer_u, peer_v, ln_g, ln_b):
    bp = x_prompt.shape[0]
    dt = x_prompt.dtype
    conv0 = jnp.zeros((N_A_LAYERS, bp, D_CONV - 1, CONV_DIM), dt)
    ssm0 = jnp.zeros((N_A_LAYERS, bp, SSM_HEADS, SSM_HEAD_DIM, SSM_STATE), jnp.float32)
    k0 = jnp.zeros((bp, 0, ATT_HEADS, ATT_HEAD_DIM), dt)
    f0 = jnp.zeros((bp, 0, ATT_HEADS), jnp.float32)
    y_prompt, p_conv, p_ssm, p_k, p_v, p_logf = trunk(
        x_prompt, conv0, ssm0, k0, k0, f0,
        a_w_in, a_conv_w, a_conv_b, a_dt_bias, a_A_log, a_D, a_norm_w, a_w_out,
        kv_w, kv_b_f, b_w_qg, b_w_o, peer_w_q, peer_subkeys, peer_u, peer_v, ln_g, ln_b)
    y_sample, s_conv, s_ssm, s_k, s_v, s_logf = trunk(
        x_sample, state_conv, state_ssm, cache_k, cache_v, cache_logf,
        a_w_in, a_conv_w, a_conv_b, a_dt_bias, a_A_log, a_D, a_norm_w, a_w_out,
        kv_w, kv_b_f, b_w_qg, b_w_o, peer_w_q, peer_subkeys, peer_u, peer_v, ln_g, ln_b)
    return (y_prompt, y_sample, p_ssm, p_conv, p_k, p_v, p_logf, s_ssm, s_conv, s_k, s_v, s_logf)
```

```python
import functools
import math

import numpy as np
import jax
import jax.numpy as jnp
from jax import lax
from jax.experimental import pallas as pl
from jax.experimental.pallas import tpu as pltpu

F32 = jnp.float32
BF16 = jnp.bfloat16
I32 = jnp.int32

CHUNK = 64
PEER_TOPK = 16
RMS_EPS = 1e-5
LN_EPS = 1e-5
LANES = 128
VMEM_TABLE_LIMIT = 56 * 1024 * 1024
NT = (((1,), (1,)), ((), ()))


def _pick(n, pref):
    if n <= pref:
        return n
    for b in range(pref, 7, -1):
        if n % b == 0 and b % 8 == 0:
            return b
    return n


def _layer_norm(v, g, b):
    mu = jnp.mean(v, -1, keepdims=True)
    d = v - mu
    var = jnp.mean(d * d, -1, keepdims=True)
    return d * lax.rsqrt(var + LN_EPS) * g + b


def _cparams(sem, vmem=None):
    kw = dict(dimension_semantics=sem)
    if vmem is not None:
        kw["vmem_limit_bytes"] = vmem
    return pltpu.CompilerParams(**kw)


def _mm_kernel(x_ref, w_ref, o_ref):
    o_ref[...] = jnp.dot(x_ref[...].astype(BF16), w_ref[...], preferred_element_type=F32)


def matmul(x, w, bm=512, bn=1024):
    T, K = x.shape
    N = w.shape[1]
    bm = _pick(T, bm)
    bn = bn if N % bn == 0 else N
    return pl.pallas_call(
        _mm_kernel,
        grid=(N // bn, T // bm),
        in_specs=[pl.BlockSpec((bm, K), lambda j, i: (i, 0)),
                  pl.BlockSpec((K, bn), lambda j, i: (0, j))],
        out_specs=pl.BlockSpec((bm, bn), lambda j, i: (i, j)),
        out_shape=jax.ShapeDtypeStruct((T, N), F32),
        compiler_params=_cparams(("arbitrary", "arbitrary")),
        name="matmul",
    )(x, w)


def _mm_res_ln_kernel(a_ref, w_ref, x_ref, g_ref, b_ref, o_ref, *, alpha):
    y = jnp.dot(a_ref[...].astype(BF16), w_ref[...], preferred_element_type=F32)
    o_ref[...] = _layer_norm(alpha * x_ref[...] + y, g_ref[...], b_ref[...])


def matmul_res_ln(a, w, x, g, b, alpha, bm=512):
    T, K = a.shape
    D = w.shape[1]
    bm = _pick(T, bm)
    return pl.pallas_call(
        functools.partial(_mm_res_ln_kernel, alpha=alpha),
        grid=(T // bm,),
        in_specs=[pl.BlockSpec((bm, K), lambda i: (i, 0)),
                  pl.BlockSpec((K, D), lambda i: (0, 0)),
                  pl.BlockSpec((bm, D), lambda i: (i, 0)),
                  pl.BlockSpec((1, D), lambda i: (0, 0)),
                  pl.BlockSpec((1, D), lambda i: (0, 0))],
        out_specs=pl.BlockSpec((bm, D), lambda i: (i, 0)),
        out_shape=jax.ShapeDtypeStruct((T, D), F32),
        compiler_params=_cparams(("arbitrary",)),
        name="matmul_res_ln",
    )(a, w, x, g, b)


def _conv_kernel(first_ref, seq_ref, u_ref, init_ref, w_ref, b_ref, o_ref, st_ref, ext_ref, *, dconv):
    c = pl.program_id(0)
    nprev = dconv - 1

    @pl.when(first_ref[c] == 1)
    def _():
        ext_ref[8 - nprev:8, :] = init_ref[0]

    u = u_ref[...]
    ext_ref[8:8 + CHUNK, :] = u
    w = w_ref[...]
    acc = b_ref[...] + u * w[dconv - 1:dconv]
    for k in range(nprev):
        acc = acc + ext_ref[8 - nprev + k:8 - nprev + k + CHUNK, :] * w[k:k + 1]
    o_ref[...] = acc * jax.nn.sigmoid(acc)
    tail = ext_ref[8 + CHUNK - nprev:8 + CHUNK, :]
    st_ref[0] = tail
    ext_ref[8 - nprev:8, :] = tail


def conv_silu(u, init, w, b, first, seq):
    T, C = u.shape
    nseq, nprev, _ = init.shape
    dconv = nprev + 1
    grid_spec = pltpu.PrefetchScalarGridSpec(
        num_scalar_prefetch=2,
        grid=(T // CHUNK,),
        in_specs=[pl.BlockSpec((CHUNK, C), lambda c, f, s: (c, 0)),
                  pl.BlockSpec((1, nprev, C), lambda c, f, s: (s[c], 0, 0)),
                  pl.BlockSpec((dconv, C), lambda c, f, s: (0, 0)),
                  pl.BlockSpec((1, C), lambda c, f, s: (0, 0))],
        out_specs=[pl.BlockSpec((CHUNK, C), lambda c, f, s: (c, 0)),
                   pl.BlockSpec((1, nprev, C), lambda c, f, s: (s[c], 0, 0))],
        scratch_shapes=[pltpu.VMEM((8 + CHUNK, C), F32)],
    )
    return pl.pallas_call(
        functools.partial(_conv_kernel, dconv=dconv),
        grid_spec=grid_spec,
        out_shape=[jax.ShapeDtypeStruct((T, C), F32), jax.ShapeDtypeStruct((nseq, nprev, C), F32)],
        compiler_params=_cparams(("arbitrary",)),
        name="conv_silu",
    )(first, seq, u, init, w, b)


def _ssd_kernel(first_ref, seq_ref, xbc_ref, dt_ref, bias_ref, alog_ref, dsk_ref, init_ref,
                y_ref, fin_ref, st_ref, *, H, P, N, G):
    c = pl.program_id(0)

    @pl.when(first_ref[c] == 1)
    def _():
        st_ref[...] = init_ref[0]

    Q = CHUNK
    DI = H * P
    R = H // G
    per = LANES // P
    hi = lax.Precision.HIGHEST
    dt = jax.nn.softplus(dt_ref[...] + bias_ref[...])
    a = dt * (-jnp.exp(alog_ref[...]))
    row = lax.broadcasted_iota(I32, (Q, Q), 0)
    col = lax.broadcasted_iota(I32, (Q, Q), 1)
    tril = row >= col
    a_cs = jnp.dot(tril.astype(F32), a, precision=hi, preferred_element_type=F32)
    eye_l = (lax.broadcasted_iota(I32, (LANES, LANES), 0) == lax.broadcasted_iota(I32, (LANES, LANES), 1)).astype(F32)
    a_cs_t = lax.dot_general(eye_l, a_cs, NT, precision=hi, preferred_element_type=F32)
    last = a_cs[Q - 1:Q, :]
    exp_cs = jnp.exp(a_cs)
    decay = jnp.exp(last - a_cs)
    chunk_decay = jnp.exp(last)
    dsk = dsk_ref[...]
    eye_n = (lax.broadcasted_iota(I32, (N, N), 0) == lax.broadcasted_iota(I32, (N, N), 1)).astype(BF16)

    for g in range(G):
        b16 = xbc_ref[:, DI + g * N:DI + (g + 1) * N].astype(BF16)
        c16 = xbc_ref[:, DI + (G + g) * N:DI + (G + g + 1) * N].astype(BF16)
        cb = lax.dot_general(c16, b16, NT, preferred_element_type=F32)
        b16_t = lax.dot_general(eye_n, b16, NT, preferred_element_type=F32).astype(BF16)
        for r0 in range(0, R, per):
            h0 = g * R + r0
            xs_w = xbc_ref[:, h0 * P:h0 * P + LANES]
            ys = []
            for hh in range(per):
                h = h0 + hh
                xs = xs_w[:, hh * P:(hh + 1) * P]
                xdt = xs * dt[:, h:h + 1]
                seg = a_cs[:, h:h + 1] - a_cs_t[h:h + 1, :]
                lmat = jnp.where(tril, jnp.exp(jnp.where(tril, seg, 0.0)), 0.0)
                m16 = (cb * lmat).astype(BF16)
                y = jnp.dot(m16, xdt.astype(BF16), preferred_element_type=F32)
                st = st_ref[h]
                y = y + jnp.dot(c16, st.astype(BF16), preferred_element_type=F32) * exp_cs[:, h:h + 1]
                ys.append(y + xs * dsk[:, h:h + 1])
                xw = (xdt * decay[:, h:h + 1]).astype(BF16)
                st_ref[h] = st * chunk_decay[:, h:h + 1] + jnp.dot(b16_t, xw, preferred_element_type=F32)
            y_ref[:, h0 * P:h0 * P + LANES] = jnp.concatenate(ys, axis=1)
    fin_ref[0] = st_ref[...]


def ssd_scan(xbc, dt_raw, dt_bias, a_log, d_skip, init_t, first, seq, H, P, N, G):
    T, C = xbc.shape
    nseq = init_t.shape[0]
    DI = H * P
    kern = functools.partial(_ssd_kernel, H=H, P=P, N=N, G=G)
    grid_spec = pltpu.PrefetchScalarGridSpec(
        num_scalar_prefetch=2,
        grid=(T // CHUNK,),
        in_specs=[pl.BlockSpec((CHUNK, C), lambda c, f, s: (c, 0)),
                  pl.BlockSpec((CHUNK, LANES), lambda c, f, s: (c, 0)),
                  pl.BlockSpec((1, LANES), lambda c, f, s: (0, 0)),
                  pl.BlockSpec((1, LANES), lambda c, f, s: (0, 0)),
                  pl.BlockSpec((1, LANES), lambda c, f, s: (0, 0)),
                  pl.BlockSpec((1, H, N, P), lambda c, f, s: (s[c], 0, 0, 0))],
        out_specs=[pl.BlockSpec((CHUNK, DI), lambda c, f, s: (c, 0)),
                   pl.BlockSpec((1, H, N, P), lambda c, f, s: (s[c], 0, 0, 0))],
        scratch_shapes=[pltpu.VMEM((H, N, P), F32)],
    )
    return pl.pallas_call(
        kern,
        grid_spec=grid_spec,
        out_shape=[jax.ShapeDtypeStruct((T, DI), F32), jax.ShapeDtypeStruct((nseq, H, N, P), F32)],
        compiler_params=_cparams(("arbitrary",)),
        name="ssd_scan",
    )(first, seq, xbc, dt_raw, dt_bias, a_log, d_skip, init_t)


def _mamba_out_kernel(y_ref, z_ref, x_ref, nw_ref, w_ref, g_ref, b_ref, o_ref, *, groups, alpha):
    z = z_ref[...]
    h = y_ref[...] * (z * jax.nn.sigmoid(z))
    gs = h.shape[1] // groups
    parts = []
    for gi in range(groups):
        hg = h[:, gi * gs:(gi + 1) * gs]
        parts.append(hg * lax.rsqrt(jnp.mean(hg * hg, -1, keepdims=True) + RMS_EPS))
    hn = jnp.concatenate(parts, axis=1) * nw_ref[...]
    out = jnp.dot(hn.astype(BF16), w_ref[...], preferred_element_type=F32)
    o_ref[...] = _layer_norm(alpha * x_ref[...] + out, g_ref[...], b_ref[...])


def mamba_out(y, z, x, norm_w, w_out, g, b, groups, alpha, bm=256):
    T, DI = y.shape
    D = x.shape[1]
    bm = _pick(T, bm)
    return pl.pallas_call(
        functools.partial(_mamba_out_kernel, groups=groups, alpha=alpha),
        grid=(T // bm,),
        in_specs=[pl.BlockSpec((bm, DI), lambda i: (i, 0)),
                  pl.BlockSpec((bm, DI), lambda i: (i, 0)),
                  pl.BlockSpec((bm, D), lambda i: (i, 0)),
                  pl.BlockSpec((1, DI), lambda i: (0, 0)),
                  pl.BlockSpec((DI, D), lambda i: (0, 0)),
                  pl.BlockSpec((1, D), lambda i: (0, 0)),
                  pl.BlockSpec((1, D), lambda i: (0, 0))],
        out_specs=pl.BlockSpec((bm, D), lambda i: (i, 0)),
        out_shape=jax.ShapeDtypeStruct((T, D), F32),
        compiler_params=_cparams(("arbitrary",)),
        name="mamba_out",
    )(y, z, x, norm_w, w_out, g, b)


def _peer_topk_kernel(x_ref, wq_ref, sk_ref, idx_ref, gate_ref, *, nk, half, rows_per_expert):
    q = jnp.dot(x_ref[...].astype(BF16), wq_ref[...], preferred_element_type=F32)
    bt = q.shape[0]
    K = PEER_TOPK
    iota_k = lax.broadcasted_iota(I32, (nk, bt), 0).astype(F32)
    neg = jnp.float32(-jnp.inf)
    svs, sis = [], []
    for j in range(2):
        qc = q[:, j * half:(j + 1) * half].astype(BF16)
        s = lax.dot_general(sk_ref[j], qc, NT, preferred_element_type=F32)
        sv, si = [], []
        for _ in range(K):
            m = jnp.max(s, axis=0, keepdims=True)
            am = jnp.min(jnp.where(s == m, iota_k, float(nk)), axis=0, keepdims=True)
            sv.append(m)
            si.append(am)
            s = jnp.where(iota_k == am, neg, s)
        svs.append(sv)
        sis.append(si)
    sv1 = jnp.concatenate(svs[1], axis=0)
    si1 = jnp.concatenate(sis[1], axis=0)
    cand = jnp.concatenate([svs[0][a] + sv1 for a in range(K)], axis=0)
    cid = jnp.concatenate([sis[0][a] * float(nk) + si1 for a in range(K)], axis=0)
    iota_c = lax.broadcasted_iota(I32, (K * K, bt), 0).astype(F32)
    best, eid = [], []
    for _ in range(K):
        m = jnp.max(cand, axis=0, keepdims=True)
        pos = jnp.min(jnp.where(cand == m, iota_c, float(K * K)), axis=0, keepdims=True)
        hit = iota_c == pos
        eid.append(jnp.max(jnp.where(hit, cid, -1.0), axis=0, keepdims=True))
        best.append(m)
        cand = jnp.where(hit, neg, cand)
    bestv = jnp.concatenate(best, axis=0)
    p = jnp.exp(bestv - bestv[0:1])
    gate_ref[...] = p / jnp.sum(p, axis=0, keepdims=True)
    idx_ref[...] = jnp.concatenate(eid, axis=0).astype(I32) * rows_per_expert


def peer_topk(x, wq, sk, rows_per_expert, bt=256):
    T, D = x.shape
    nk, half = sk.shape[1], sk.shape[2]
    nh = wq.shape[1] // (2 * half)
    K = PEER_TOPK
    bt = _pick(T, bt)
    kern = functools.partial(_peer_topk_kernel, nk=nk, half=half, rows_per_expert=rows_per_expert)
    return pl.pallas_call(
        kern,
        grid=(T // bt, nh),
        in_specs=[pl.BlockSpec((bt, D), lambda i, h: (i, 0)),
                  pl.BlockSpec((D, 2 * half), lambda i, h: (0, h)),
                  pl.BlockSpec((2, nk, half), lambda i, h: (0, 0, 0))],
        out_specs=[pl.BlockSpec((K, bt), lambda i, h: (h, i)),
                   pl.BlockSpec((K, bt), lambda i, h: (h, i))],
        out_shape=[jax.ShapeDtypeStruct((nh * K, T), I32), jax.ShapeDtypeStruct((nh * K, T), F32)],
        compiler_params=_cparams(("arbitrary", "arbitrary")),
        name="peer_topk",
    )(x, wq, sk)


def _gather_rows(idx_ref, t, tab_ref, tile_ref, nsel, p):
    for k in range(nsel):
        e = pl.multiple_of(idx_ref[k, t], p)
        tile_ref[pl.ds(k * p, p), :] = tab_ref[pl.ds(e, p), :]


def _unpack_rows(tile_ref, nsel, p):
    lo, hi = [], []
    for j in range(p):
        w = tile_ref[pl.ds(j, nsel, stride=p), :] if p > 1 else tile_ref[...]
        lo.append(lax.bitcast_convert_type(w << 16, F32))
        hi.append(lax.bitcast_convert_type(w & jnp.int32(-65536), F32))
    return jnp.concatenate(lo + hi, axis=1).astype(BF16)


def _peer_u_kernel(idx_ref, x_ref, gate_ref, tab_ref, act_ref, tile_ref, *, nsel, p):
    xb = x_ref[...].astype(BF16)
    bt = xb.shape[0]
    lane = lax.broadcasted_iota(I32, (nsel, bt), 1)

    def body(t, acc):
        _gather_rows(idx_ref, t, tab_ref, tile_ref, nsel, p)
        rows = _unpack_rows(tile_ref, nsel, p)
        res = lax.dot_general(rows, xb, NT, preferred_element_type=F32)
        return jnp.where(lane == t, res, acc)

    hpre = lax.fori_loop(0, bt, body, jnp.zeros((nsel, bt), F32))
    act = 0.5 * hpre * (1.0 + lax.erf(hpre * (2.0 ** -0.5))) * gate_ref[...]
    act_ref[...] = act.T


def peer_u_phase(idx, x, gates, tab, p, bt=128):
    nsel, T = idx.shape
    D = x.shape[1]
    bt = _pick(T, bt)
    kern = functools.partial(_peer_u_kernel, nsel=nsel, p=p)
    return pl.pallas_call(
        kern,
        grid=(T // bt,),
        in_specs=[pl.BlockSpec((nsel, bt), lambda i: (0, i), memory_space=pltpu.SMEM),
                  pl.BlockSpec((bt, D), lambda i: (i, 0)),
                  pl.BlockSpec((nsel, bt), lambda i: (0, i)),
                  pl.BlockSpec(tab.shape, lambda i: (0, 0), pipeline_mode=pl.Buffered(1))],
        out_specs=pl.BlockSpec((bt, nsel), lambda i: (i, 0)),
        out_shape=jax.ShapeDtypeStruct((T, nsel), F32),
        scratch_shapes=[pltpu.VMEM((nsel * p, LANES), I32)],
        compiler_params=_cparams(("arbitrary",), VMEM_TABLE_LIMIT),
        name="peer_u",
    )(idx, x, gates, tab)


def _peer_v_kernel(idx_ref, act_ref, x_ref, tab_ref, g_ref, b_ref, o_ref, tile_ref, out_ref, *, nsel, p, alpha):
    bt = x_ref.shape[0]

    def body(t, carry):
        _gather_rows(idx_ref, t, tab_ref, tile_ref, nsel, p)
        rows = _unpack_rows(tile_ref, nsel, p)
        a8 = jnp.broadcast_to(act_ref[pl.ds(t, 1), :], (8, nsel)).astype(BF16)
        o8 = jnp.dot(a8, rows, preferred_element_type=F32)
        out_ref[pl.ds(t, 1), :] = o8[0:1]
        return carry

    lax.fori_loop(0, bt, body, 0)
    o_ref[...] = _layer_norm(alpha * x_ref[...] + out_ref[...], g_ref[...], b_ref[...])


def peer_v_phase(idx, act, x, tab, g, b, p, alpha, bt=128):
    nsel, T = idx.shape
    D = x.shape[1]
    bt = _pick(T, bt)
    kern = functools.partial(_peer_v_kernel, nsel=nsel, p=p, alpha=alpha)
    return pl.pallas_call(
        kern,
        grid=(T // bt,),
        in_specs=[pl.BlockSpec((nsel, bt), lambda i: (0, i), memory_space=pltpu.SMEM),
                  pl.BlockSpec((bt, nsel), lambda i: (i, 0)),
                  pl.BlockSpec((bt, D), lambda i: (i, 0)),
                  pl.BlockSpec(tab.shape, lambda i: (0, 0), pipeline_mode=pl.Buffered(1)),
                  pl.BlockSpec((1, D), lambda i: (0, 0)),
                  pl.BlockSpec((1, D), lambda i: (0, 0))],
        out_specs=pl.BlockSpec((bt, D), lambda i: (i, 0)),
        out_shape=jax.ShapeDtypeStruct((T, D), F32),
        scratch_shapes=[pltpu.VMEM((nsel * p, LANES), I32), pltpu.VMEM((bt, D), F32)],
        compiler_params=_cparams(("arbitrary",), VMEM_TABLE_LIMIT),
        name="peer_v",
    )(idx, act, x, tab, g, b)


def pack_table(tab):
    E, D = tab.shape
    bits = lax.bitcast_convert_type(tab.astype(BF16), jnp.uint16).astype(jnp.uint32)
    words = bits[:, :D // 2] | (bits[:, D // 2:] << 16)
    return lax.bitcast_convert_type(words, I32).reshape(E * (D // 256), LANES)


def _kv_kernel(x_ref, wkv_ref, wf_ref, wft_ref, bf_ref, bft_ref, k_ref, v_ref, k16_ref, v16_ref, lf_ref, lft_ref, *, aw):
    xb = x_ref[...].astype(BF16)
    kv = jnp.dot(xb, wkv_ref[...], preferred_element_type=F32)
    k = kv[:, :aw]
    v = kv[:, aw:]
    k_ref[...] = k
    v_ref[...] = v
    k16_ref[...] = k.astype(BF16)
    v16_ref[...] = v.astype(BF16)
    f = jnp.dot(xb, wf_ref[...], preferred_element_type=F32) + bf_ref[...]
    lf_ref[...] = jax.nn.log_sigmoid(f)
    ft = lax.dot_general(wft_ref[...], xb, NT, preferred_element_type=F32) + bft_ref[...]
    lft_ref[...] = jax.nn.log_sigmoid(ft)


def shared_kv(x, wkv, wf, wft, bf, bft, bm=512):
    T, D = x.shape
    aw = wkv.shape[1] // 2
    H = wf.shape[1]
    bm = _pick(T, bm)
    outs = [jax.ShapeDtypeStruct((T, aw), F32), jax.ShapeDtypeStruct((T, aw), F32),
            jax.ShapeDtypeStruct((T, aw), BF16), jax.ShapeDtypeStruct((T, aw), BF16),
            jax.ShapeDtypeStruct((T, H), F32), jax.ShapeDtypeStruct((H, T), F32)]
    row = lambda i: (i, 0)
    fixed = lambda i: (0, 0)
    return pl.pallas_call(
        functools.partial(_kv_kernel, aw=aw),
        grid=(T // bm,),
        in_specs=[pl.BlockSpec((bm, D), row), pl.BlockSpec((D, 2 * aw), fixed), pl.BlockSpec((D, H), fixed),
                  pl.BlockSpec((H, D), fixed), pl.BlockSpec((1, H), fixed), pl.BlockSpec((H, 1), fixed)],
        out_specs=[pl.BlockSpec((bm, aw), row), pl.BlockSpec((bm, aw), row), pl.BlockSpec((bm, aw), row),
                   pl.BlockSpec((bm, aw), row), pl.BlockSpec((bm, H), row), pl.BlockSpec((H, bm), lambda i: (0, i))],
        out_shape=outs,
        compiler_params=_cparams(("arbitrary",)),
        name="shared_kv",
    )(x, wkv, wf, wft, bf, bft)


def _cumsum_kernel(lf_ref, lft_ref, ck_ref, ckt_ref):
    L = lf_ref.shape[1]
    B = LANES
    hi = lax.Precision.HIGHEST
    r = lax.broadcasted_iota(I32, (B, B), 0)
    c = lax.broadcasted_iota(I32, (B, B), 1)
    lower = (r >= c).astype(F32)
    upper = (r <= c).astype(F32)
    H = lf_ref.shape[2]
    carry = jnp.zeros((1, H), F32)
    carry_t = jnp.zeros((H, 1), F32)
    for j in range(L // B):
        blk = jnp.dot(lower, lf_ref[0, j * B:(j + 1) * B, :], precision=hi, preferred_element_type=F32) + carry
        ck_ref[0, j * B:(j + 1) * B, :] = blk
        carry = blk[B - 1:B, :]
        blk_t = jnp.dot(lft_ref[0, :, j * B:(j + 1) * B], upper, precision=hi, preferred_element_type=F32) + carry_t
        ckt_ref[0, :, j * B:(j + 1) * B] = blk_t
        carry_t = blk_t[:, B - 1:B]


def forget_cumsum(lf, lft):
    nseq, L, H = lf.shape
    return pl.pallas_call(
        _cumsum_kernel,
        grid=(nseq,),
        in_specs=[pl.BlockSpec((1, L, H), lambda s: (s, 0, 0)), pl.BlockSpec((1, H, L), lambda s: (s, 0, 0))],
        out_specs=[pl.BlockSpec((1, L, H), lambda s: (s, 0, 0)), pl.BlockSpec((1, H, L), lambda s: (s, 0, 0))],
        out_shape=[jax.ShapeDtypeStruct((nseq, L, H), F32), jax.ShapeDtypeStruct((nseq, H, L), F32)],
        compiler_params=_cparams(("arbitrary",)),
        name="forget_cumsum",
    )(lf, lft)


def _fox_kernel(q_ref, g_ref, k_ref, v_ref, cq_ref, ckt_ref, o_ref, *, bq, bk, dh, off, scale):
    hp = pl.program_id(1)
    qi = pl.program_id(2)
    q2 = q_ref[...] * scale
    lane = lax.broadcasted_iota(I32, (bq, 2 * dh), 1)
    qpos = off + qi * bq + lax.broadcasted_iota(I32, (bq, bk), 0)
    kiota = lax.broadcasted_iota(I32, (bq, bk), 1)
    nkb = (off + (qi + 1) * bq + bk - 1) // bk
    cq_all = cq_ref[0]
    head_lane = lax.broadcasted_iota(I32, cq_all.shape, 1)
    outs = []
    for hh in range(2):
        sel = (lane >= dh) if hh else (lane < dh)
        qh = jnp.where(sel, q2, 0.0).astype(BF16)
        cq = jnp.sum(jnp.where(head_lane == 2 * hp + hh, cq_all, 0.0), axis=1, keepdims=True)

        def body(kb, carry, qh=qh, cq=cq, hh=hh):
            m, l, acc = carry
            ks = pl.multiple_of(kb * bk, bk)
            s = lax.dot_general(qh, k_ref[0, pl.ds(ks, bk), :], NT, preferred_element_type=F32)
            ck = ckt_ref[0, 2 * hp + hh, :, pl.ds(ks, bk)]
            s = s + cq - ck
            s = jnp.where(kiota + ks <= qpos, s, -jnp.inf)
            m_new = jnp.maximum(m, jnp.max(s, axis=1, keepdims=True))
            alpha = jnp.exp(m - m_new)
            pexp = jnp.exp(s - m_new)
            l = alpha * l + jnp.sum(pexp, axis=1, keepdims=True)
            acc = alpha * acc + jnp.dot(pexp.astype(BF16), v_ref[0, pl.ds(ks, bk), :], preferred_element_type=F32)
            return m_new, l, acc

        m0 = jnp.full((bq, 1), -jnp.inf, F32)
        l0 = jnp.zeros((bq, 1), F32)
        a0 = jnp.zeros((bq, 2 * dh), F32)
        _, l, acc = lax.fori_loop(0, nkb, body, (m0, l0, a0))
        outs.append(acc / l)
    o = jnp.where(lane < dh, outs[0], outs[1])
    o_ref[...] = o * jax.nn.sigmoid(g_ref[...])


def fox_attention(qg, k16, v16, ck, ckt, tok0, nseq, L, Lk, dh, bq, bk):
    aw = k16.shape[2]
    Lkp = k16.shape[1]
    H = ck.shape[2]
    off = Lk - L
    nq = L // bq
    tb0 = tok0 // bq
    npair = aw // (2 * dh)
    kern = functools.partial(_fox_kernel, bq=bq, bk=bk, dh=dh, off=off, scale=dh ** -0.5)
    return pl.pallas_call(
        kern,
        grid=(nseq, npair, nq),
        in_specs=[pl.BlockSpec((bq, 2 * dh), lambda s, h, i: (tb0 + s * nq + i, h)),
                  pl.BlockSpec((bq, 2 * dh), lambda s, h, i: (tb0 + s * nq + i, npair + h)),
                  pl.BlockSpec((1, Lkp, 2 * dh), lambda s, h, i: (s, 0, h)),
                  pl.BlockSpec((1, Lkp, 2 * dh), lambda s, h, i: (s, 0, h)),
                  pl.BlockSpec((1, bq, H), lambda s, h, i: (s, off // bq + i, 0)),
                  pl.BlockSpec((1, H, 1, Lkp), lambda s, h, i: (s, 0, 0, 0))],
        out_specs=pl.BlockSpec((bq, 2 * dh), lambda s, h, i: (s * nq + i, h)),
        out_shape=jax.ShapeDtypeStruct((nseq * L, aw), F32),
        compiler_params=_cparams(("arbitrary", "arbitrary", "arbitrary")),
        name="fox_attention",
    )(qg, qg, k16, v16, ck, ckt.reshape(nseq, H, 1, Lkp))


def _pad_lanes(v, n=LANES):
    v = v.reshape(1, -1).astype(F32)
    return jnp.pad(v, ((0, 0), (0, n - v.shape[1])))


def _round_up(n, m):
    return (n + m - 1) // m * m


def kernel(x_prompt, x_sample, state_ssm, state_conv, cache_k, cache_v, cache_logf, a_w_in, a_conv_w, a_conv_b, a_dt_bias, a_A_log, a_D, a_norm_w, a_w_out, kv_w, kv_b_f, b_w_qg, b_w_o, peer_w_q, peer_subkeys, peer_u, peer_v, ln_g, ln_b):
    Bp, S, D = x_prompt.shape
    Bs, Ss, _ = x_sample.shape
    depth = ln_g.shape[0]
    n_a = a_w_in.shape[0]
    H = a_dt_bias.shape[1]
    DI = a_w_out.shape[1]
    P = DI // H
    N = state_ssm.shape[-1]
    C = a_conv_w.shape[2]
    G = (C - DI) // (2 * N)
    past = cache_k.shape[1]
    AH, dh = cache_k.shape[2], cache_k.shape[3]
    AW = AH * dh
    alpha = (2.0 * depth) ** 0.25
    Tp, Ts = Bp * S, Bs * Ss
    T = Tp + Ts
    nseq = Bp + Bs
    assert S % CHUNK == 0 and Ss % CHUNK == 0 and H <= LANES and D % 256 == 0

    seq_np = np.concatenate([np.repeat(np.arange(Bp), S // CHUNK), Bp + np.repeat(np.arange(Bs), Ss // CHUNK)])
    first_np = np.concatenate([[1], (seq_np[1:] != seq_np[:-1]).astype(np.int64)])
    seq = jnp.asarray(seq_np, I32)
    first = jnp.asarray(first_np, I32)

    x = jnp.concatenate([x_prompt.reshape(Tp, D), x_sample.reshape(Ts, D)], axis=0)
    rows_per_expert = D // 256
    new_conv, new_ssm = [], []
    k_f32 = v_f32 = lf = None

    for i in range(depth):
        g1, b1 = ln_g[i, 0].reshape(1, D), ln_b[i, 0].reshape(1, D)
        g2, b2 = ln_g[i, 1].reshape(1, D), ln_b[i, 1].reshape(1, D)
        if i < n_a:
            w_in = a_w_in[i].astype(BF16)
            z = matmul(x, w_in[:, :DI])
            xbc_raw = matmul(x, w_in[:, DI:DI + C])
            w_dt = jnp.pad(w_in[:, DI + C:], ((0, 0), (0, LANES - H)))
            dt_raw = matmul(x, w_dt)
            conv0 = jnp.concatenate([jnp.zeros((Bp,) + state_conv.shape[2:], F32), state_conv[i]], axis=0)
            xbc, conv_new = conv_silu(xbc_raw, conv0, a_conv_w[i], a_conv_b[i].reshape(1, C), first, seq)
            ssm0 = jnp.concatenate([jnp.zeros((Bp, H, P, N), F32), state_ssm[i]], axis=0)
            y, ssm_new = ssd_scan(xbc, dt_raw, _pad_lanes(a_dt_bias[i]), _pad_lanes(a_A_log[i]), _pad_lanes(a_D[i]),
                                  jnp.swapaxes(ssm0, 2, 3), first, seq, H, P, N, G)
            new_conv.append(conv_new)
            new_ssm.append(jnp.swapaxes(ssm_new, 2, 3))
            x = mamba_out(y, z, x, a_norm_w[i].reshape(1, DI), a_w_out[i].astype(BF16), g1, b1, G, alpha)
        else:
            j = i - n_a
            qg = matmul(x, b_w_qg[j].astype(BF16))
            o_p = fox_attention(qg, kp16, vp16, ck_p, ckt_p, 0, Bp, S, S, dh, bq_p, bk_p)
            o_s = fox_attention(qg, ks16, vs16, ck_s, ckt_s, Tp, Bs, Ss, past + Ss, dh, bq_s, bk_s)
            x = matmul_res_ln(jnp.concatenate([o_p, o_s], axis=0), b_w_o[j].astype(BF16), x, g1, b1, alpha)

        idx, gates = peer_topk(x, peer_w_q[i].astype(BF16), peer_subkeys[i].astype(BF16), rows_per_expert)
        act = peer_u_phase(idx, x, gates, pack_table(peer_u[i]), rows_per_expert)
        x = peer_v_phase(idx, act, x, pack_table(peer_v[i]), g2, b2, rows_per_expert, alpha)

        if i == n_a - 1:
            wkv = kv_w[:, :2 * AW].astype(BF16)
            wf = kv_w[:, 2 * AW:].astype(BF16)
            k_f32, v_f32, k16, v16, lf, lft = shared_kv(x, wkv, wf, wf.T, kv_b_f.reshape(1, AH), kv_b_f.reshape(AH, 1))
            bq_p = _pick(S, 128)
            bk_p = _pick(S, 512)
            kp16 = k16[:Tp].reshape(Bp, S, AW)
            vp16 = v16[:Tp].reshape(Bp, S, AW)
            Sp = _round_up(S, LANES)
            lf_p = jnp.pad(lf[:Tp].reshape(Bp, S, AH), ((0, 0), (0, Sp - S), (0, 0)))
            lft_p = jnp.pad(jnp.swapaxes(lft[:, :Tp].reshape(AH, Bp, S), 0, 1), ((0, 0), (0, 0), (0, Sp - S)))
            ck_p, ckt_p = forget_cumsum(lf_p, lft_p)
            Lk = past + Ss
            bq_s = _pick(Ss, 128)
            bk_s = LANES
            Lkp = _round_up(Lk, bk_s)
            padk = ((0, 0), (0, Lkp - Lk), (0, 0))
            ks16 = jnp.pad(jnp.concatenate([cache_k.reshape(Bs, past, AW).astype(BF16), k16[Tp:].reshape(Bs, Ss, AW)], axis=1), padk)
            vs16 = jnp.pad(jnp.concatenate([cache_v.reshape(Bs, past, AW).astype(BF16), v16[Tp:].reshape(Bs, Ss, AW)], axis=1), padk)
            lf_s = jnp.pad(jnp.concatenate([cache_logf.astype(F32), lf[Tp:].reshape(Bs, Ss, AH)], axis=1), padk)
            lft_new = jnp.swapaxes(lft[:, Tp:].reshape(AH, Bs, Ss), 0, 1)
            lft_s = jnp.pad(jnp.concatenate([jnp.swapaxes(cache_logf.astype(F32), 1, 2), lft_new], axis=2),
                            ((0, 0), (0, 0), (0, Lkp - Lk)))
            ck_s, ckt_s = forget_cumsum(lf_s, lft_s)

    conv_all = jnp.stack(new_conv)
    ssm_all = jnp.stack(new_ssm)
    return (x[:Tp].reshape(Bp, S, D), x[Tp:].reshape(Bs, Ss, D),
            ssm_all[:, :Bp], conv_all[:, :Bp],
            k_f32[:Tp].reshape(Bp, S, AH, dh), v_f32[:Tp].reshape(Bp, S, AH, dh), lf[:Tp].reshape(Bp, S, AH),
            ssm_all[:, Bp:], conv_all[:, Bp:],
            k_f32[Tp:].reshape(Bs, Ss, AH, dh), v_f32[Tp:].reshape(Bs, Ss, AH, dh), lf[Tp:].reshape(Bs, Ss, AH))
```

```python
import functools
import math

import numpy as np
import jax
import jax.numpy as jnp
from jax import lax
from jax.experimental import pallas as pl
from jax.experimental.pallas import tpu as pltpu

F32 = jnp.float32
BF16 = jnp.bfloat16
I32 = jnp.int32

CHUNK = 64
PEER_TOPK = 16
RMS_EPS = 1e-5
LN_EPS = 1e-5
LANES = 128
VMEM_TABLE_LIMIT = 56 * 1024 * 1024
NT = (((1,), (1,)), ((), ()))


def _pick(n, pref):
    if n <= pref:
        return n
    for b in range(pref, 7, -1):
        if n % b == 0 and b % 8 == 0:
            return b
    return n


def _layer_norm(v, g, b):
    mu = jnp.mean(v, -1, keepdims=True)
    d = v - mu
    var = jnp.mean(d * d, -1, keepdims=True)
    return d * lax.rsqrt(var + LN_EPS) * g + b


def _cparams(sem, vmem=None):
    kw = dict(dimension_semantics=sem)
    if vmem is not None:
        kw["vmem_limit_bytes"] = vmem
    return pltpu.CompilerParams(**kw)


def _mm_kernel(x_ref, w_ref, o_ref):
    o_ref[...] = jnp.dot(x_ref[...].astype(BF16), w_ref[...], preferred_element_type=F32)


def matmul(x, w, bm=512, bn=1024):
    T, K = x.shape
    N = w.shape[1]
    bm = _pick(T, bm)
    bn = bn if N % bn == 0 else N
    return pl.pallas_call(
        _mm_kernel,
        grid=(N // bn, T // bm),
        in_specs=[pl.BlockSpec((bm, K), lambda j, i: (i, 0)),
                  pl.BlockSpec((K, bn), lambda j, i: (0, j))],
        out_specs=pl.BlockSpec((bm, bn), lambda j, i: (i, j)),
        out_shape=jax.ShapeDtypeStruct((T, N), F32),
        compiler_params=_cparams(("arbitrary", "arbitrary")),
        name="matmul",
    )(x, w)


def _mm_res_ln_kernel(a_ref, w_ref, x_ref, g_ref, b_ref, o_ref, *, alpha):
    y = jnp.dot(a_ref[...].astype(BF16), w_ref[...], preferred_element_type=F32)
    o_ref[...] = _layer_norm(alpha * x_ref[...] + y, g_ref[...], b_ref[...])


def matmul_res_ln(a, w, x, g, b, alpha, bm=512):
    T, K = a.shape
    D = w.shape[1]
    bm = _pick(T, bm)
    return pl.pallas_call(
        functools.partial(_mm_res_ln_kernel, alpha=alpha),
        grid=(T // bm,),
        in_specs=[pl.BlockSpec((bm, K), lambda i: (i, 0)),
                  pl.BlockSpec((K, D), lambda i: (0, 0)),
                  pl.BlockSpec((bm, D), lambda i: (i, 0)),
                  pl.BlockSpec((1, D), lambda i: (0, 0)),
                  pl.BlockSpec((1, D), lambda i: (0, 0))],
        out_specs=pl.BlockSpec((bm, D), lambda i: (i, 0)),
        out_shape=jax.ShapeDtypeStruct((T, D), F32),
        compiler_params=_cparams(("arbitrary",)),
        name="matmul_res_ln",
    )(a, w, x, g, b)


def _conv_kernel(first_ref, seq_ref, u_ref, init_ref, w_ref, b_ref, o_ref, st_ref, ext_ref, *, dconv):
    c = pl.program_id(0)
    nprev = dconv - 1

    @pl.when(first_ref[c] == 1)
    def _():
        ext_ref[8 - nprev:8, :] = init_ref[0]

    u = u_ref[...]
    ext_ref[8:8 + CHUNK, :] = u
    w = w_ref[...]
    acc = b_ref[...] + u * w[dconv - 1:dconv]
    for k in range(nprev):
        acc = acc + ext_ref[8 - nprev + k:8 - nprev + k + CHUNK, :] * w[k:k + 1]
    o_ref[...] = acc * jax.nn.sigmoid(acc)
    tail = ext_ref[8 + CHUNK - nprev:8 + CHUNK, :]
    st_ref[0] = tail
    ext_ref[8 - nprev:8, :] = tail


def conv_silu(u, init, w, b, first, seq):
    T, C = u.shape
    nseq, nprev, _ = init.shape
    dconv = nprev + 1
    grid_spec = pltpu.PrefetchScalarGridSpec(
        num_scalar_prefetch=2,
        grid=(T // CHUNK,),
        in_specs=[pl.BlockSpec((CHUNK, C), lambda c, f, s: (c, 0)),
                  pl.BlockSpec((1, nprev, C), lambda c, f, s: (s[c], 0, 0)),
                  pl.BlockSpec((dconv, C), lambda c, f, s: (0, 0)),
                  pl.BlockSpec((1, C), lambda c, f, s: (0, 0))],
        out_specs=[pl.BlockSpec((CHUNK, C), lambda c, f, s: (c, 0)),
                   pl.BlockSpec((1, nprev, C), lambda c, f, s: (s[c], 0, 0))],
        scratch_shapes=[pltpu.VMEM((8 + CHUNK, C), F32)],
    )
    return pl.pallas_call(
        functools.partial(_conv_kernel, dconv=dconv),
        grid_spec=grid_spec,
        out_shape=[jax.ShapeDtypeStruct((T, C), F32), jax.ShapeDtypeStruct((nseq, nprev, C), F32)],
        compiler_params=_cparams(("arbitrary",)),
        name="conv_silu",
    )(first, seq, u, init, w, b)


def _ssd_kernel(first_ref, seq_ref, xbc_ref, dt_ref, bias_ref, alog_ref, dsk_ref, init_ref,
                y_ref, fin_ref, st_ref, *, H, P, N, G):
    c = pl.program_id(0)

    @pl.when(first_ref[c] == 1)
    def _():
        st_ref[...] = init_ref[0]

    Q = CHUNK
    DI = H * P
    R = H // G
    per = LANES // P
    hi = lax.Precision.HIGHEST
    dt = jax.nn.softplus(dt_ref[...] + bias_ref[...])
    a = dt * (-jnp.exp(alog_ref[...]))
    row = lax.broadcasted_iota(I32, (Q, Q), 0)
    col = lax.broadcasted_iota(I32, (Q, Q), 1)
    tril = row >= col
    a_cs = jnp.dot(tril.astype(F32), a, precision=hi, preferred_element_type=F32)
    eye_l = (lax.broadcasted_iota(I32, (LANES, LANES), 0) == lax.broadcasted_iota(I32, (LANES, LANES), 1)).astype(F32)
    a_cs_t = lax.dot_general(eye_l, a_cs, NT, precision=hi, preferred_element_type=F32)
    last = a_cs[Q - 1:Q, :]
    exp_cs = jnp.exp(a_cs)
    decay = jnp.exp(last - a_cs)
    chunk_decay = jnp.exp(last)
    dsk = dsk_ref[...]
    eye_n = (lax.broadcasted_iota(I32, (N, N), 0) == lax.broadcasted_iota(I32, (N, N), 1)).astype(BF16)

    for g in range(G):
        b16 = xbc_ref[:, DI + g * N:DI + (g + 1) * N].astype(BF16)
        c16 = xbc_ref[:, DI + (G + g) * N:DI + (G + g + 1) * N].astype(BF16)
        cb = lax.dot_general(c16, b16, NT, preferred_element_type=F32)
        b16_t = lax.dot_general(eye_n, b16, NT, preferred_element_type=F32).astype(BF16)
        for r0 in range(0, R, per):
            h0 = g * R + r0
            xs_w = xbc_ref[:, h0 * P:h0 * P + LANES]
            ys = []
            for hh in range(per):
                h = h0 + hh
                xs = xs_w[:, hh * P:(hh + 1) * P]
                xdt = xs * dt[:, h:h + 1]
                seg = a_cs[:, h:h + 1] - a_cs_t[h:h + 1, :]
                lmat = jnp.where(tril, jnp.exp(jnp.where(tril, seg, 0.0)), 0.0)
                m16 = (cb * lmat).astype(BF16)
                y = jnp.dot(m16, xdt.astype(BF16), preferred_element_type=F32)
                st = st_ref[h]
                y = y + jnp.dot(c16, st.astype(BF16), preferred_element_type=F32) * exp_cs[:, h:h + 1]
                ys.append(y + xs * dsk[:, h:h + 1])
                xw = (xdt * decay[:, h:h + 1]).astype(BF16)
                st_ref[h] = st * chunk_decay[:, h:h + 1] + jnp.dot(b16_t, xw, preferred_element_type=F32)
            y_ref[:, h0 * P:h0 * P + LANES] = jnp.concatenate(ys, axis=1)
    fin_ref[0] = st_ref[...]


def ssd_scan(xbc, dt_raw, dt_bias, a_log, d_skip, init_t, first, seq, H, P, N, G):
    T, C = xbc.shape
    nseq = init_t.shape[0]
    DI = H * P
    kern = functools.partial(_ssd_kernel, H=H, P=P, N=N, G=G)
    grid_spec = pltpu.PrefetchScalarGridSpec(
        num_scalar_prefetch=2,
        grid=(T // CHUNK,),
        in_specs=[pl.BlockSpec((CHUNK, C), lambda c, f, s: (c, 0)),
                  pl.BlockSpec((CHUNK, LANES), lambda c, f, s: (c, 0)),
                  pl.BlockSpec((1, LANES), lambda c, f, s: (0, 0)),
                  pl.BlockSpec((1, LANES), lambda c, f, s: (0, 0)),
                  pl.BlockSpec((1, LANES), lambda c, f, s: (0, 0)),
                  pl.BlockSpec((1, H, N, P), lambda c, f, s: (s[c], 0, 0, 0))],
        out_specs=[pl.BlockSpec((CHUNK, DI), lambda c, f, s: (c, 0)),
                   pl.BlockSpec((1, H, N, P), lambda c, f, s: (s[c], 0, 0, 0))],
        scratch_shapes=[pltpu.VMEM((H, N, P), F32)],
    )
    return pl.pallas_call(
        kern,
        grid_spec=grid_spec,
        out_shape=[jax.ShapeDtypeStruct((T, DI), F32), jax.ShapeDtypeStruct((nseq, H, N, P), F32)],
        compiler_params=_cparams(("arbitrary",)),
        name="ssd_scan",
    )(first, seq, xbc, dt_raw, dt_bias, a_log, d_skip, init_t)


def _mamba_out_kernel(y_ref, z_ref, x_ref, nw_ref, w_ref, g_ref, b_ref, o_ref, *, groups, alpha):
    z = z_ref[...]
    h = y_ref[...] * (z * jax.nn.sigmoid(z))
    gs = h.shape[1] // groups
    parts = []
    for gi in range(groups):
        hg = h[:, gi * gs:(gi + 1) * gs]
        parts.append(hg * lax.rsqrt(jnp.mean(hg * hg, -1, keepdims=True) + RMS_EPS))
    hn = jnp.concatenate(parts, axis=1) * nw_ref[...]
    out = jnp.dot(hn.astype(BF16), w_ref[...], preferred_element_type=F32)
    o_ref[...] = _layer_norm(alpha * x_ref[...] + out, g_ref[...], b_ref[...])


def mamba_out(y, z, x, norm_w, w_out, g, b, groups, alpha, bm=256):
    T, DI = y.shape
    D = x.shape[1]
    bm = _pick(T, bm)
    return pl.pallas_call(
        functools.partial(_mamba_out_kernel, groups=groups, alpha=alpha),
        grid=(T // bm,),
        in_specs=[pl.BlockSpec((bm, DI), lambda i: (i, 0)),
                  pl.BlockSpec((bm, DI), lambda i: (i, 0)),
                  pl.BlockSpec((bm, D), lambda i: (i, 0)),
                  pl.BlockSpec((1, DI), lambda i: (0, 0)),
                  pl.BlockSpec((DI, D), lambda i: (0, 0)),
                  pl.BlockSpec((1, D), lambda i: (0, 0)),
                  pl.BlockSpec((1, D), lambda i: (0, 0))],
        out_specs=pl.BlockSpec((bm, D), lambda i: (i, 0)),
        out_shape=jax.ShapeDtypeStruct((T, D), F32),
        compiler_params=_cparams(("arbitrary",)),
        name="mamba_out",
    )(y, z, x, norm_w, w_out, g, b)


def _peer_topk_kernel(x_ref, wq_ref, sk_ref, idx_ref, gate_ref, *, nk, half, rows_per_expert):
    q = jnp.dot(x_ref[...].astype(BF16), wq_ref[...], preferred_element_type=F32)
    bt = q.shape[0]
    K = PEER_TOPK
    cw = min(bt, LANES)
    neg = jnp.float32(-jnp.inf)
    iota_k = lax.broadcasted_iota(I32, (nk, cw), 0).astype(F32)

    def fiota(rows):
        return lax.broadcasted_iota(I32, (rows, cw), 0).astype(F32)

    for c in range(bt // cw):
        sv, si = [], []
        for j in range(2):
            qc = q[c * cw:(c + 1) * cw, j * half:(j + 1) * half].astype(BF16)
            s = lax.dot_general(sk_ref[j], qc, NT, preferred_element_type=F32)
            v_rows, i_rows = [], []
            for _ in range(K):
                m = jnp.max(s, axis=0, keepdims=True)
                am = jnp.min(jnp.where(s == m, iota_k, float(nk)), axis=0, keepdims=True)
                v_rows.append(m)
                i_rows.append(am)
                s = jnp.where(iota_k == am, neg, s)
            sv.append(jnp.concatenate(v_rows, axis=0))
            si.append(jnp.concatenate(i_rows, axis=0))
        vals, ids, flats = [], [], []
        a = 0
        while a < K and K // (a + 1) >= 2:
            n_a = K // (a + 1)
            rows = _round_up(n_a, 8)
            b_pos = fiota(rows)
            vals.append(jnp.where(b_pos < n_a, sv[0][a:a + 1] + sv[1][:rows], neg))
            ids.append(si[0][a:a + 1] * float(nk) + si[1][:rows])
            flats.append(b_pos + float(a * K))
            a += 1
        if a < K:
            vals.append(sv[0][a:K] + sv[1][0:1])
            ids.append(si[0][a:K] * float(nk) + si[1][0:1])
            flats.append((fiota(K - a) + float(a)) * float(K))
        cand = jnp.concatenate(vals, axis=0)
        cid = jnp.concatenate(ids, axis=0)
        flat = jnp.concatenate(flats, axis=0)
        best, eid = [], []
        for _ in range(K):
            m = jnp.max(cand, axis=0, keepdims=True)
            pos = jnp.min(jnp.where(cand == m, flat, float(K * K)), axis=0, keepdims=True)
            hit = flat == pos
            eid.append(jnp.max(jnp.where(hit, cid, -1.0), axis=0, keepdims=True))
            best.append(m)
            cand = jnp.where(hit, neg, cand)
        bestv = jnp.concatenate(best, axis=0)
        pr = jnp.exp(bestv - bestv[0:1])
        gate_ref[:, c * cw:(c + 1) * cw] = pr / jnp.sum(pr, axis=0, keepdims=True)
        idx_ref[:, c * cw:(c + 1) * cw] = jnp.concatenate(eid, axis=0).astype(I32) * rows_per_expert


def peer_topk(x, wq, sk, rows_per_expert, bt=256):
    T, D = x.shape
    nk, half = sk.shape[1], sk.shape[2]
    nh = wq.shape[1] // (2 * half)
    K = PEER_TOPK
    bt = _pick(T, bt)
    kern = functools.partial(_peer_topk_kernel, nk=nk, half=half, rows_per_expert=rows_per_expert)
    return pl.pallas_call(
        kern,
        grid=(T // bt, nh),
        in_specs=[pl.BlockSpec((bt, D), lambda i, h: (i, 0)),
                  pl.BlockSpec((D, 2 * half), lambda i, h: (0, h)),
                  pl.BlockSpec((2, nk, half), lambda i, h: (0, 0, 0))],
        out_specs=[pl.BlockSpec((K, bt), lambda i, h: (h, i)),
                   pl.BlockSpec((K, bt), lambda i, h: (h, i))],
        out_shape=[jax.ShapeDtypeStruct((nh * K, T), I32), jax.ShapeDtypeStruct((nh * K, T), F32)],
        compiler_params=_cparams(("arbitrary", "arbitrary")),
        name="peer_topk",
    )(x, wq, sk)


GROUP = 8


def _gather_rows(idx_ref, t0, tab_ref, tile_ref, nsel, p):
    rows = [idx_ref.at[t0 + i] for i in range(GROUP)]
    for k in range(nsel):
        for i in range(GROUP):
            e = pl.multiple_of(rows[i][k], p)
            tile_ref[pl.ds((i * nsel + k) * p, p), :] = tab_ref[pl.ds(e, p), :]


def _unpack_rows(tile_ref, rows_ref, n, p):
    for j in range(p):
        w = tile_ref[pl.ds(j, n, stride=p), :] if p > 1 else tile_ref[...]
        rows_ref[:, j * LANES:(j + 1) * LANES] = lax.bitcast_convert_type(w << 16, F32).astype(BF16)
        rows_ref[:, (p + j) * LANES:(p + j + 1) * LANES] = (
            lax.bitcast_convert_type(w & jnp.int32(-65536), F32).astype(BF16))


def _peer_u_kernel(idx_ref, x_ref, gate_ref, tab_ref, act_ref, tile_ref, rows_ref, xb_ref, *, nsel, p):
    bt = x_ref.shape[0]
    xb_ref[...] = x_ref[...].astype(BF16)
    lane = lax.broadcasted_iota(I32, (nsel, bt), 1)

    _gather_rows(idx_ref, 0, tab_ref, tile_ref.at[0], nsel, p)

    def body(j, acc):
        for s in range(2):
            t0 = (2 * j + s) * GROUP
            _gather_rows(idx_ref, jnp.minimum(t0 + GROUP, bt - GROUP), tab_ref, tile_ref.at[1 - s], nsel, p)
            _unpack_rows(tile_ref.at[s], rows_ref.at[s], GROUP * nsel, p)
            res = lax.dot_general(rows_ref[s], xb_ref[...], NT, preferred_element_type=F32)
            for i in range(GROUP):
                acc = jnp.where(lane == t0 + i, res[i * nsel:(i + 1) * nsel], acc)
        return acc

    hpre = lax.fori_loop(0, bt // (2 * GROUP), body, jnp.zeros((nsel, bt), F32))
    act = 0.5 * hpre * (1.0 + lax.erf(hpre * (2.0 ** -0.5))) * gate_ref[...]
    act_ref[...] = act.T


def peer_u_phase(idx_t, x, gates, tab, p, bt=256):
    T, nsel = idx_t.shape
    D = x.shape[1]
    bt = _pick(T, bt)
    kern = functools.partial(_peer_u_kernel, nsel=nsel, p=p)
    return pl.pallas_call(
        kern,
        grid=(T // bt,),
        in_specs=[pl.BlockSpec((bt, nsel), lambda i: (i, 0), memory_space=pltpu.SMEM),
                  pl.BlockSpec((bt, D), lambda i: (i, 0)),
                  pl.BlockSpec((nsel, bt), lambda i: (0, i)),
                  pl.BlockSpec(tab.shape, lambda i: (0, 0), pipeline_mode=pl.Buffered(1))],
        out_specs=pl.BlockSpec((bt, nsel), lambda i: (i, 0)),
        out_shape=jax.ShapeDtypeStruct((T, nsel), F32),
        scratch_shapes=[pltpu.VMEM((2, GROUP * nsel * p, LANES), I32), pltpu.VMEM((2, GROUP * nsel, D), BF16),
                        pltpu.VMEM((bt, D), BF16)],
        compiler_params=_cparams(("arbitrary",), VMEM_TABLE_LIMIT),
        name="peer_u",
    )(idx_t, x, gates, tab)


def _peer_v_kernel(idx_ref, act_ref, x_ref, tab_ref, g_ref, b_ref, o_ref, tile_ref, rows_ref, out_ref, *, nsel, p, alpha):
    bt = x_ref.shape[0]
    n = GROUP * nsel
    col = lax.broadcasted_iota(I32, (GROUP, n), 1)
    lo = lax.broadcasted_iota(I32, (GROUP, n), 0) * nsel
    own = (col >= lo) & (col < lo + nsel)

    _gather_rows(idx_ref, 0, tab_ref, tile_ref.at[0], nsel, p)

    def body(j, carry):
        for s in range(2):
            t0 = pl.multiple_of((2 * j + s) * GROUP, GROUP)
            _gather_rows(idx_ref, jnp.minimum(t0 + GROUP, bt - GROUP), tab_ref, tile_ref.at[1 - s], nsel, p)
            _unpack_rows(tile_ref.at[s], rows_ref.at[s], n, p)
            a = act_ref[pl.ds(t0, GROUP), :]
            a_blk = jnp.where(own, jnp.concatenate([a] * GROUP, axis=1), 0.0).astype(BF16)
            out_ref[pl.ds(t0, GROUP), :] = jnp.dot(a_blk, rows_ref[s], preferred_element_type=F32)
        return carry

    lax.fori_loop(0, bt // (2 * GROUP), body, 0)
    o_ref[...] = _layer_norm(alpha * x_ref[...] + out_ref[...], g_ref[...], b_ref[...])


def peer_v_phase(idx_t, act, x, tab, g, b, p, alpha, bt=256):
    T, nsel = idx_t.shape
    D = x.shape[1]
    bt = _pick(T, bt)
    kern = functools.partial(_peer_v_kernel, nsel=nsel, p=p, alpha=alpha)
    return pl.pallas_call(
        kern,
        grid=(T // bt,),
        in_specs=[pl.BlockSpec((bt, nsel), lambda i: (i, 0), memory_space=pltpu.SMEM),
                  pl.BlockSpec((bt, nsel), lambda i: (i, 0)),
                  pl.BlockSpec((bt, D), lambda i: (i, 0)),
                  pl.BlockSpec(tab.shape, lambda i: (0, 0), pipeline_mode=pl.Buffered(1)),
                  pl.BlockSpec((1, D), lambda i: (0, 0)),
                  pl.BlockSpec((1, D), lambda i: (0, 0))],
        out_specs=pl.BlockSpec((bt, D), lambda i: (i, 0)),
        out_shape=jax.ShapeDtypeStruct((T, D), F32),
        scratch_shapes=[pltpu.VMEM((2, GROUP * nsel * p, LANES), I32), pltpu.VMEM((2, GROUP * nsel, D), BF16),
                        pltpu.VMEM((bt, D), F32)],
        compiler_params=_cparams(("arbitrary",), VMEM_TABLE_LIMIT),
        name="peer_v",
    )(idx_t, act, x, tab, g, b)


def pack_table(tab):
    E, D = tab.shape
    bits = lax.bitcast_convert_type(tab.astype(BF16), jnp.uint16).astype(jnp.uint32)
    words = bits[:, :D // 2] | (bits[:, D // 2:] << 16)
    return lax.bitcast_convert_type(words, I32).reshape(E * (D // 256), LANES)


def _kv_kernel(x_ref, wkv_ref, wf_ref, wft_ref, bf_ref, bft_ref, k_ref, v_ref, k16_ref, v16_ref, lf_ref, lft_ref, *, aw):
    xb = x_ref[...].astype(BF16)
    kv = jnp.dot(xb, wkv_ref[...], preferred_element_type=F32)
    k = kv[:, :aw]
    v = kv[:, aw:]
    k_ref[...] = k
    v_ref[...] = v
    k16_ref[...] = k.astype(BF16)
    v16_ref[...] = v.astype(BF16)
    f = jnp.dot(xb, wf_ref[...], preferred_element_type=F32) + bf_ref[...]
    lf_ref[...] = jax.nn.log_sigmoid(f)
    ft = lax.dot_general(wft_ref[...], xb, NT, preferred_element_type=F32) + bft_ref[...]
    lft_ref[...] = jax.nn.log_sigmoid(ft)


def shared_kv(x, wkv, wf, wft, bf, bft, bm=512):
    T, D = x.shape
    aw = wkv.shape[1] // 2
    H = wf.shape[1]
    bm = _pick(T, bm)
    outs = [jax.ShapeDtypeStruct((T, aw), F32), jax.ShapeDtypeStruct((T, aw), F32),
            jax.ShapeDtypeStruct((T, aw), BF16), jax.ShapeDtypeStruct((T, aw), BF16),
            jax.ShapeDtypeStruct((T, H), F32), jax.ShapeDtypeStruct((H, T), F32)]
    row = lambda i: (i, 0)
    fixed = lambda i: (0, 0)
    return pl.pallas_call(
        functools.partial(_kv_kernel, aw=aw),
        grid=(T // bm,),
        in_specs=[pl.BlockSpec((bm, D), row), pl.BlockSpec((D, 2 * aw), fixed), pl.BlockSpec((D, H), fixed),
                  pl.BlockSpec((H, D), fixed), pl.BlockSpec((1, H), fixed), pl.BlockSpec((H, 1), fixed)],
        out_specs=[pl.BlockSpec((bm, aw), row), pl.BlockSpec((bm, aw), row), pl.BlockSpec((bm, aw), row),
                   pl.BlockSpec((bm, aw), row), pl.BlockSpec((bm, H), row), pl.BlockSpec((H, bm), lambda i: (0, i))],
        out_shape=outs,
        compiler_params=_cparams(("arbitrary",)),
        name="shared_kv",
    )(x, wkv, wf, wft, bf, bft)


def _cumsum_kernel(lf_ref, lft_ref, ck_ref, ckt_ref):
    L = lf_ref.shape[1]
    B = LANES
    hi = lax.Precision.HIGHEST
    r = lax.broadcasted_iota(I32, (B, B), 0)
    c = lax.broadcasted_iota(I32, (B, B), 1)
    lower = (r >= c).astype(F32)
    upper = (r <= c).astype(F32)
    H = lf_ref.shape[2]
    carry = jnp.zeros((1, H), F32)
    carry_t = jnp.zeros((H, 1), F32)
    for j in range(L // B):
        blk = jnp.dot(lower, lf_ref[0, j * B:(j + 1) * B, :], precision=hi, preferred_element_type=F32) + carry
        ck_ref[0, j * B:(j + 1) * B, :] = blk
        carry = blk[B - 1:B, :]
        blk_t = jnp.dot(lft_ref[0, :, j * B:(j + 1) * B], upper, precision=hi, preferred_element_type=F32) + carry_t
        ckt_ref[0, :, j * B:(j + 1) * B] = blk_t
        carry_t = blk_t[:, B - 1:B]


def forget_cumsum(lf, lft):
    nseq, L, H = lf.shape
    return pl.pallas_call(
        _cumsum_kernel,
        grid=(nseq,),
        in_specs=[pl.BlockSpec((1, L, H), lambda s: (s, 0, 0)), pl.BlockSpec((1, H, L), lambda s: (s, 0, 0))],
        out_specs=[pl.BlockSpec((1, L, H), lambda s: (s, 0, 0)), pl.BlockSpec((1, H, L), lambda s: (s, 0, 0))],
        out_shape=[jax.ShapeDtypeStruct((nseq, L, H), F32), jax.ShapeDtypeStruct((nseq, H, L), F32)],
        compiler_params=_cparams(("arbitrary",)),
        name="forget_cumsum",
    )(lf, lft)


def _fox_kernel(q_ref, g_ref, k_ref, v_ref, cq_ref, ckt_ref, o_ref, *, bq, bk, dh, off, scale):
    hp = pl.program_id(1)
    qi = pl.program_id(2)
    q2 = q_ref[...] * scale
    lane = lax.broadcasted_iota(I32, (bq, 2 * dh), 1)
    qpos = off + qi * bq + lax.broadcasted_iota(I32, (bq, bk), 0)
    kiota = lax.broadcasted_iota(I32, (bq, bk), 1)
    nkb = (off + (qi + 1) * bq + bk - 1) // bk
    cq_all = cq_ref[0]
    head_lane = lax.broadcasted_iota(I32, cq_all.shape, 1)
    qh, cq = [], []
    for hh in range(2):
        sel = (lane >= dh) if hh else (lane < dh)
        qh.append(jnp.where(sel, q2, 0.0).astype(BF16))
        cq.append(jnp.sum(jnp.where(head_lane == 2 * hp + hh, cq_all, 0.0), axis=1, keepdims=True))

    def body(kb, carry):
        ks = pl.multiple_of(kb * bk, bk)
        kblk = k_ref[0, pl.ds(ks, bk), :]
        vblk = v_ref[0, pl.ds(ks, bk), :]
        causal = kiota + ks <= qpos
        new = []
        for hh in range(2):
            m, l, acc = carry[hh]
            s = lax.dot_general(qh[hh], kblk, NT, preferred_element_type=F32)
            ck = ckt_ref[0, 2 * hp + hh, :, pl.ds(ks, bk)]
            s = jnp.where(causal, s + cq[hh] - ck, -jnp.inf)
            m_new = jnp.maximum(m, jnp.max(s, axis=1, keepdims=True))
            alpha = jnp.exp(m - m_new)
            pexp = jnp.exp(s - m_new)
            l = alpha * l + jnp.sum(pexp, axis=1, keepdims=True)
            acc = alpha * acc + jnp.dot(pexp.astype(BF16), vblk, preferred_element_type=F32)
            new.append((m_new, l, acc))
        return tuple(new)

    init = (jnp.full((bq, 1), -jnp.inf, F32), jnp.zeros((bq, 1), F32), jnp.zeros((bq, 2 * dh), F32))
    (_, l0, a0), (_, l1, a1) = lax.fori_loop(0, nkb, body, (init, init))
    o = jnp.where(lane < dh, a0 / l0, a1 / l1)
    o_ref[...] = o * jax.nn.sigmoid(g_ref[...])


def fox_attention(qg, k16, v16, ck, ckt, tok0, nseq, L, Lk, dh, bq, bk):
    aw = k16.shape[2]
    Lkp = k16.shape[1]
    H = ck.shape[2]
    off = Lk - L
    nq = L // bq
    tb0 = tok0 // bq
    npair = aw // (2 * dh)
    kern = functools.partial(_fox_kernel, bq=bq, bk=bk, dh=dh, off=off, scale=dh ** -0.5)
    return pl.pallas_call(
        kern,
        grid=(nseq, npair, nq),
        in_specs=[pl.BlockSpec((bq, 2 * dh), lambda s, h, i: (tb0 + s * nq + i, h)),
                  pl.BlockSpec((bq, 2 * dh), lambda s, h, i: (tb0 + s * nq + i, npair + h)),
                  pl.BlockSpec((1, Lkp, 2 * dh), lambda s, h, i: (s, 0, h)),
                  pl.BlockSpec((1, Lkp, 2 * dh), lambda s, h, i: (s, 0, h)),
                  pl.BlockSpec((1, bq, H), lambda s, h, i: (s, off // bq + i, 0)),
                  pl.BlockSpec((1, H, 1, Lkp), lambda s, h, i: (s, 0, 0, 0))],
        out_specs=pl.BlockSpec((bq, 2 * dh), lambda s, h, i: (s * nq + i, h)),
        out_shape=jax.ShapeDtypeStruct((nseq * L, aw), F32),
        compiler_params=_cparams(("arbitrary", "arbitrary", "arbitrary")),
        name="fox_attention",
    )(qg, qg, k16, v16, ck, ckt.reshape(nseq, H, 1, Lkp))


def _pad_lanes(v, n=LANES):
    v = v.reshape(1, -1).astype(F32)
    return jnp.pad(v, ((0, 0), (0, n - v.shape[1])))


def _round_up(n, m):
    return (n + m - 1) // m * m


def kernel(x_prompt, x_sample, state_ssm, state_conv, cache_k, cache_v, cache_logf, a_w_in, a_conv_w, a_conv_b, a_dt_bias, a_A_log, a_D, a_norm_w, a_w_out, kv_w, kv_b_f, b_w_qg, b_w_o, peer_w_q, peer_subkeys, peer_u, peer_v, ln_g, ln_b):
    Bp, S, D = x_prompt.shape
    Bs, Ss, _ = x_sample.shape
    depth = ln_g.shape[0]
    n_a = a_w_in.shape[0]
    H = a_dt_bias.shape[1]
    DI = a_w_out.shape[1]
    P = DI // H
    N = state_ssm.shape[-1]
    C = a_conv_w.shape[2]
    G = (C - DI) // (2 * N)
    past = cache_k.shape[1]
    AH, dh = cache_k.shape[2], cache_k.shape[3]
    AW = AH * dh
    alpha = (2.0 * depth) ** 0.25
    Tp, Ts = Bp * S, Bs * Ss
    T = Tp + Ts
    nseq = Bp + Bs
    assert S % CHUNK == 0 and Ss % CHUNK == 0 and H <= LANES and D % 256 == 0

    seq_np = np.concatenate([np.repeat(np.arange(Bp), S // CHUNK), Bp + np.repeat(np.arange(Bs), Ss // CHUNK)])
    first_np = np.concatenate([[1], (seq_np[1:] != seq_np[:-1]).astype(np.int64)])
    seq = jnp.asarray(seq_np, I32)
    first = jnp.asarray(first_np, I32)

    x = jnp.concatenate([x_prompt.reshape(Tp, D), x_sample.reshape(Ts, D)], axis=0)
    rows_per_expert = D // 256
    new_conv, new_ssm = [], []
    k_f32 = v_f32 = lf = None

    for i in range(depth):
        g1, b1 = ln_g[i, 0].reshape(1, D), ln_b[i, 0].reshape(1, D)
        g2, b2 = ln_g[i, 1].reshape(1, D), ln_b[i, 1].reshape(1, D)
        if i < n_a:
            w_in = a_w_in[i].astype(BF16)
            z = matmul(x, w_in[:, :DI])
            xbc_raw = matmul(x, w_in[:, DI:DI + C])
            w_dt = jnp.pad(w_in[:, DI + C:], ((0, 0), (0, LANES - H)))
            dt_raw = matmul(x, w_dt)
            conv0 = jnp.concatenate([jnp.zeros((Bp,) + state_conv.shape[2:], F32), state_conv[i]], axis=0)
            xbc, conv_new = conv_silu(xbc_raw, conv0, a_conv_w[i], a_conv_b[i].reshape(1, C), first, seq)
            ssm0 = jnp.concatenate([jnp.zeros((Bp, H, P, N), F32), state_ssm[i]], axis=0)
            y, ssm_new = ssd_scan(xbc, dt_raw, _pad_lanes(a_dt_bias[i]), _pad_lanes(a_A_log[i]), _pad_lanes(a_D[i]),
                                  jnp.swapaxes(ssm0, 2, 3), first, seq, H, P, N, G)
            new_conv.append(conv_new)
            new_ssm.append(jnp.swapaxes(ssm_new, 2, 3))
            x = mamba_out(y, z, x, a_norm_w[i].reshape(1, DI), a_w_out[i].astype(BF16), g1, b1, G, alpha)
        else:
            j = i - n_a
            qg = matmul(x, b_w_qg[j].astype(BF16))
            o_p = fox_attention(qg, kp16, vp16, ck_p, ckt_p, 0, Bp, S, S, dh, bq_p, bk_p)
            o_s = fox_attention(qg, ks16, vs16, ck_s, ckt_s, Tp, Bs, Ss, past + Ss, dh, bq_s, bk_s)
            x = matmul_res_ln(jnp.concatenate([o_p, o_s], axis=0), b_w_o[j].astype(BF16), x, g1, b1, alpha)

        idx, gates = peer_topk(x, peer_w_q[i].astype(BF16), peer_subkeys[i].astype(BF16), rows_per_expert)
        idx_t = idx.T
        act = peer_u_phase(idx_t, x, gates, pack_table(peer_u[i]), rows_per_expert)
        x = peer_v_phase(idx_t, act, x, pack_table(peer_v[i]), g2, b2, rows_per_expert, alpha)

        if i == n_a - 1:
            wkv = kv_w[:, :2 * AW].astype(BF16)
            wf = kv_w[:, 2 * AW:].astype(BF16)
            k_f32, v_f32, k16, v16, lf, lft = shared_kv(x, wkv, wf, wf.T, kv_b_f.reshape(1, AH), kv_b_f.reshape(AH, 1))
            bq_p = _pick(S, 128)
            bk_p = _pick(S, 512)
            kp16 = k16[:Tp].reshape(Bp, S, AW)
            vp16 = v16[:Tp].reshape(Bp, S, AW)
            Sp = _round_up(S, LANES)
            lf_p = jnp.pad(lf[:Tp].reshape(Bp, S, AH), ((0, 0), (0, Sp - S), (0, 0)))
            lft_p = jnp.pad(jnp.swapaxes(lft[:, :Tp].reshape(AH, Bp, S), 0, 1), ((0, 0), (0, 0), (0, Sp - S)))
            ck_p, ckt_p = forget_cumsum(lf_p, lft_p)
            Lk = past + Ss
            bq_s = _pick(Ss, 128)
            Lkp = _round_up(Lk, LANES)
            bk_s = max(b for b in range(LANES, 4 * LANES + 1, LANES) if Lkp % b == 0)
            padk = ((0, 0), (0, Lkp - Lk), (0, 0))
            ks16 = jnp.pad(jnp.concatenate([cache_k.reshape(Bs, past, AW).astype(BF16), k16[Tp:].reshape(Bs, Ss, AW)], axis=1), padk)
            vs16 = jnp.pad(jnp.concatenate([cache_v.reshape(Bs, past, AW).astype(BF16), v16[Tp:].reshape(Bs, Ss, AW)], axis=1), padk)
            lf_s = jnp.pad(jnp.concatenate([cache_logf.astype(F32), lf[Tp:].reshape(Bs, Ss, AH)], axis=1), padk)
            lft_new = jnp.swapaxes(lft[:, Tp:].reshape(AH, Bs, Ss), 0, 1)
            lft_s = jnp.pad(jnp.concatenate([jnp.swapaxes(cache_logf.astype(F32), 1, 2), lft_new], axis=2),
                            ((0, 0), (0, 0), (0, Lkp - Lk)))
            ck_s, ckt_s = forget_cumsum(lf_s, lft_s)

    conv_all = jnp.stack(new_conv)
    ssm_all = jnp.stack(new_ssm)
    return (x[:Tp].reshape(Bp, S, D), x[Tp:].reshape(Bs, Ss, D),
            ssm_all[:, :Bp], conv_all[:, :Bp],
            k_f32[:Tp].reshape(Bp, S, AH, dh), v_f32[:Tp].reshape(Bp, S, AH, dh), lf[:Tp].reshape(Bp, S, AH),
            ssm_all[:, Bp:], conv_all[:, Bp:],
            k_f32[Tp:].reshape(Bs, Ss, AH, dh), v_f32[Tp:].reshape(Bs, Ss, AH, dh), lf[Tp:].reshape(Bs, Ss, AH))
```

```python
import functools
import math

import numpy as np
import jax
import jax.numpy as jnp
from jax import lax
from jax.experimental import pallas as pl
from jax.experimental.pallas import tpu as pltpu

F32 = jnp.float32
BF16 = jnp.bfloat16
I32 = jnp.int32

CHUNK = 64
PEER_TOPK = 16
RMS_EPS = 1e-5
LN_EPS = 1e-5
LANES = 128
VMEM_TABLE_LIMIT = 56 * 1024 * 1024
NT = (((1,), (1,)), ((), ()))


def _pick(n, pref):
    if n <= pref:
        return n
    for b in range(pref, 7, -1):
        if n % b == 0 and b % 8 == 0:
            return b
    return n


def _layer_norm(v, g, b):
    mu = jnp.mean(v, -1, keepdims=True)
    d = v - mu
    var = jnp.mean(d * d, -1, keepdims=True)
    return d * lax.rsqrt(var + LN_EPS) * g + b


def _cparams(sem, vmem=None):
    kw = dict(dimension_semantics=sem)
    if vmem is not None:
        kw["vmem_limit_bytes"] = vmem
    return pltpu.CompilerParams(**kw)


def _mm_kernel(x_ref, w_ref, o_ref):
    o_ref[...] = jnp.dot(x_ref[...].astype(BF16), w_ref[...], preferred_element_type=F32)


def matmul(x, w, bm=512, bn=1024):
    T, K = x.shape
    N = w.shape[1]
    bm = _pick(T, bm)
    bn = bn if N % bn == 0 else N
    return pl.pallas_call(
        _mm_kernel,
        grid=(N // bn, T // bm),
        in_specs=[pl.BlockSpec((bm, K), lambda j, i: (i, 0)),
                  pl.BlockSpec((K, bn), lambda j, i: (0, j))],
        out_specs=pl.BlockSpec((bm, bn), lambda j, i: (i, j)),
        out_shape=jax.ShapeDtypeStruct((T, N), F32),
        compiler_params=_cparams(("arbitrary", "arbitrary")),
        name="matmul",
    )(x, w)


def _mm_res_ln_kernel(a_ref, w_ref, x_ref, g_ref, b_ref, o_ref, *, alpha):
    y = jnp.dot(a_ref[...].astype(BF16), w_ref[...], preferred_element_type=F32)
    o_ref[...] = _layer_norm(alpha * x_ref[...] + y, g_ref[...], b_ref[...])


def matmul_res_ln(a, w, x, g, b, alpha, bm=512):
    T, K = a.shape
    D = w.shape[1]
    bm = _pick(T, bm)
    return pl.pallas_call(
        functools.partial(_mm_res_ln_kernel, alpha=alpha),
        grid=(T // bm,),
        in_specs=[pl.BlockSpec((bm, K), lambda i: (i, 0)),
                  pl.BlockSpec((K, D), lambda i: (0, 0)),
                  pl.BlockSpec((bm, D), lambda i: (i, 0)),
                  pl.BlockSpec((1, D), lambda i: (0, 0)),
                  pl.BlockSpec((1, D), lambda i: (0, 0))],
        out_specs=pl.BlockSpec((bm, D), lambda i: (i, 0)),
        out_shape=jax.ShapeDtypeStruct((T, D), F32),
        compiler_params=_cparams(("arbitrary",)),
        name="matmul_res_ln",
    )(a, w, x, g, b)


def _conv_kernel(first_ref, seq_ref, u_ref, init_ref, w_ref, b_ref, o_ref, st_ref, ext_ref, *, dconv):
    c = pl.program_id(0)
    nprev = dconv - 1

    @pl.when(first_ref[c] == 1)
    def _():
        ext_ref[8 - nprev:8, :] = init_ref[0]

    u = u_ref[...]
    ext_ref[8:8 + CHUNK, :] = u
    w = w_ref[...]
    acc = b_ref[...] + u * w[dconv - 1:dconv]
    for k in range(nprev):
        acc = acc + ext_ref[8 - nprev + k:8 - nprev + k + CHUNK, :] * w[k:k + 1]
    o_ref[...] = acc * jax.nn.sigmoid(acc)
    tail = ext_ref[8 + CHUNK - nprev:8 + CHUNK, :]
    st_ref[0] = tail
    ext_ref[8 - nprev:8, :] = tail


def conv_silu(u, init, w, b, first, seq):
    T, C = u.shape
    nseq, nprev, _ = init.shape
    dconv = nprev + 1
    grid_spec = pltpu.PrefetchScalarGridSpec(
        num_scalar_prefetch=2,
        grid=(T // CHUNK,),
        in_specs=[pl.BlockSpec((CHUNK, C), lambda c, f, s: (c, 0)),
                  pl.BlockSpec((1, nprev, C), lambda c, f, s: (s[c], 0, 0)),
                  pl.BlockSpec((dconv, C), lambda c, f, s: (0, 0)),
                  pl.BlockSpec((1, C), lambda c, f, s: (0, 0))],
        out_specs=[pl.BlockSpec((CHUNK, C), lambda c, f, s: (c, 0)),
                   pl.BlockSpec((1, nprev, C), lambda c, f, s: (s[c], 0, 0))],
        scratch_shapes=[pltpu.VMEM((8 + CHUNK, C), F32)],
    )
    return pl.pallas_call(
        functools.partial(_conv_kernel, dconv=dconv),
        grid_spec=grid_spec,
        out_shape=[jax.ShapeDtypeStruct((T, C), F32), jax.ShapeDtypeStruct((nseq, nprev, C), F32)],
        compiler_params=_cparams(("arbitrary",)),
        name="conv_silu",
    )(first, seq, u, init, w, b)


def _ssd_kernel(first_ref, seq_ref, xbc_ref, dt_ref, bias_ref, alog_ref, dsk_ref, init_ref,
                y_ref, fin_ref, st_ref, *, H, P, N, G):
    c = pl.program_id(0)

    @pl.when(first_ref[c] == 1)
    def _():
        st_ref[...] = init_ref[0]

    Q = CHUNK
    DI = H * P
    R = H // G
    per = LANES // P
    hi = lax.Precision.HIGHEST
    dt = jax.nn.softplus(dt_ref[...] + bias_ref[...])
    a = dt * (-jnp.exp(alog_ref[...]))
    row = lax.broadcasted_iota(I32, (Q, Q), 0)
    col = lax.broadcasted_iota(I32, (Q, Q), 1)
    tril = row >= col
    a_cs = jnp.dot(tril.astype(F32), a, precision=hi, preferred_element_type=F32)
    eye_l = (lax.broadcasted_iota(I32, (LANES, LANES), 0) == lax.broadcasted_iota(I32, (LANES, LANES), 1)).astype(F32)
    a_cs_t = lax.dot_general(eye_l, a_cs, NT, precision=hi, preferred_element_type=F32)
    last = a_cs[Q - 1:Q, :]
    exp_cs = jnp.exp(a_cs)
    decay = jnp.exp(last - a_cs)
    chunk_decay = jnp.exp(last)
    dsk = dsk_ref[...]
    eye_n = (lax.broadcasted_iota(I32, (N, N), 0) == lax.broadcasted_iota(I32, (N, N), 1)).astype(BF16)

    for g in range(G):
        b16 = xbc_ref[:, DI + g * N:DI + (g + 1) * N].astype(BF16)
        c16 = xbc_ref[:, DI + (G + g) * N:DI + (G + g + 1) * N].astype(BF16)
        cb = lax.dot_general(c16, b16, NT, preferred_element_type=F32)
        b16_t = lax.dot_general(eye_n, b16, NT, preferred_element_type=F32).astype(BF16)
        for r0 in range(0, R, per):
            h0 = g * R + r0
            xs_w = xbc_ref[:, h0 * P:h0 * P + LANES]
            ys = []
            for hh in range(per):
                h = h0 + hh
                xs = xs_w[:, hh * P:(hh + 1) * P]
                xdt = xs * dt[:, h:h + 1]
                seg = a_cs[:, h:h + 1] - a_cs_t[h:h + 1, :]
                lmat = jnp.where(tril, jnp.exp(jnp.where(tril, seg, 0.0)), 0.0)
                m16 = (cb * lmat).astype(BF16)
                y = jnp.dot(m16, xdt.astype(BF16), preferred_element_type=F32)
                st = st_ref[h]
                y = y + jnp.dot(c16, st.astype(BF16), preferred_element_type=F32) * exp_cs[:, h:h + 1]
                ys.append(y + xs * dsk[:, h:h + 1])
                xw = (xdt * decay[:, h:h + 1]).astype(BF16)
                st_ref[h] = st * chunk_decay[:, h:h + 1] + jnp.dot(b16_t, xw, preferred_element_type=F32)
            y_ref[:, h0 * P:h0 * P + LANES] = jnp.concatenate(ys, axis=1)
    fin_ref[0] = st_ref[...]


def ssd_scan(xbc, dt_raw, dt_bias, a_log, d_skip, init_t, first, seq, H, P, N, G):
    T, C = xbc.shape
    nseq = init_t.shape[0]
    DI = H * P
    kern = functools.partial(_ssd_kernel, H=H, P=P, N=N, G=G)
    grid_spec = pltpu.PrefetchScalarGridSpec(
        num_scalar_prefetch=2,
        grid=(T // CHUNK,),
        in_specs=[pl.BlockSpec((CHUNK, C), lambda c, f, s: (c, 0)),
                  pl.BlockSpec((CHUNK, LANES), lambda c, f, s: (c, 0)),
                  pl.BlockSpec((1, LANES), lambda c, f, s: (0, 0)),
                  pl.BlockSpec((1, LANES), lambda c, f, s: (0, 0)),
                  pl.BlockSpec((1, LANES), lambda c, f, s: (0, 0)),
                  pl.BlockSpec((1, H, N, P), lambda c, f, s: (s[c], 0, 0, 0))],
        out_specs=[pl.BlockSpec((CHUNK, DI), lambda c, f, s: (c, 0)),
                   pl.BlockSpec((1, H, N, P), lambda c, f, s: (s[c], 0, 0, 0))],
        scratch_shapes=[pltpu.VMEM((H, N, P), F32)],
    )
    return pl.pallas_call(
        kern,
        grid_spec=grid_spec,
        out_shape=[jax.ShapeDtypeStruct((T, DI), F32), jax.ShapeDtypeStruct((nseq, H, N, P), F32)],
        compiler_params=_cparams(("arbitrary",)),
        name="ssd_scan",
    )(first, seq, xbc, dt_raw, dt_bias, a_log, d_skip, init_t)


def _mamba_out_kernel(y_ref, z_ref, x_ref, nw_ref, w_ref, g_ref, b_ref, o_ref, *, groups, alpha):
    z = z_ref[...]
    h = y_ref[...] * (z * jax.nn.sigmoid(z))
    gs = h.shape[1] // groups
    parts = []
    for gi in range(groups):
        hg = h[:, gi * gs:(gi + 1) * gs]
        parts.append(hg * lax.rsqrt(jnp.mean(hg * hg, -1, keepdims=True) + RMS_EPS))
    hn = jnp.concatenate(parts, axis=1) * nw_ref[...]
    out = jnp.dot(hn.astype(BF16), w_ref[...], preferred_element_type=F32)
    o_ref[...] = _layer_norm(alpha * x_ref[...] + out, g_ref[...], b_ref[...])


def mamba_out(y, z, x, norm_w, w_out, g, b, groups, alpha, bm=256):
    T, DI = y.shape
    D = x.shape[1]
    bm = _pick(T, bm)
    return pl.pallas_call(
        functools.partial(_mamba_out_kernel, groups=groups, alpha=alpha),
        grid=(T // bm,),
        in_specs=[pl.BlockSpec((bm, DI), lambda i: (i, 0)),
                  pl.BlockSpec((bm, DI), lambda i: (i, 0)),
                  pl.BlockSpec((bm, D), lambda i: (i, 0)),
                  pl.BlockSpec((1, DI), lambda i: (0, 0)),
                  pl.BlockSpec((DI, D), lambda i: (0, 0)),
                  pl.BlockSpec((1, D), lambda i: (0, 0)),
                  pl.BlockSpec((1, D), lambda i: (0, 0))],
        out_specs=pl.BlockSpec((bm, D), lambda i: (i, 0)),
        out_shape=jax.ShapeDtypeStruct((T, D), F32),
        compiler_params=_cparams(("arbitrary",)),
        name="mamba_out",
    )(y, z, x, norm_w, w_out, g, b)


def _peer_topk_kernel(x_ref, wq_ref, sk_ref, idx_ref, gate_ref, *, nk, half, rows_per_expert):
    q = jnp.dot(x_ref[...].astype(BF16), wq_ref[...], preferred_element_type=F32)
    bt = q.shape[0]
    K = PEER_TOPK
    cw = min(bt, LANES)
    neg = jnp.float32(-jnp.inf)
    iota_k = lax.broadcasted_iota(I32, (nk, cw), 0).astype(F32)

    def fiota(rows):
        return lax.broadcasted_iota(I32, (rows, cw), 0).astype(F32)

    for c in range(bt // cw):
        sv, si = [], []
        for j in range(2):
            qc = q[c * cw:(c + 1) * cw, j * half:(j + 1) * half].astype(BF16)
            s = lax.dot_general(sk_ref[j], qc, NT, preferred_element_type=F32)
            v_rows, i_rows = [], []
            for _ in range(K):
                m = jnp.max(s, axis=0, keepdims=True)
                am = jnp.min(jnp.where(s == m, iota_k, float(nk)), axis=0, keepdims=True)
                v_rows.append(m)
                i_rows.append(am)
                s = jnp.where(iota_k == am, neg, s)
            sv.append(jnp.concatenate(v_rows, axis=0))
            si.append(jnp.concatenate(i_rows, axis=0))
        vals, ids, flats = [], [], []
        a = 0
        while a < K and K // (a + 1) >= 2:
            n_a = K // (a + 1)
            rows = _round_up(n_a, 8)
            b_pos = fiota(rows)
            vals.append(jnp.where(b_pos < n_a, sv[0][a:a + 1] + sv[1][:rows], neg))
            ids.append(si[0][a:a + 1] * float(nk) + si[1][:rows])
            flats.append(b_pos + float(a * K))
            a += 1
        if a < K:
            vals.append(sv[0][a:K] + sv[1][0:1])
            ids.append(si[0][a:K] * float(nk) + si[1][0:1])
            flats.append((fiota(K - a) + float(a)) * float(K))
        cand = jnp.concatenate(vals, axis=0)
        cid = jnp.concatenate(ids, axis=0)
        flat = jnp.concatenate(flats, axis=0)
        best, eid = [], []
        for _ in range(K):
            m = jnp.max(cand, axis=0, keepdims=True)
            pos = jnp.min(jnp.where(cand == m, flat, float(K * K)), axis=0, keepdims=True)
            hit = flat == pos
            eid.append(jnp.max(jnp.where(hit, cid, -1.0), axis=0, keepdims=True))
            best.append(m)
            cand = jnp.where(hit, neg, cand)
        bestv = jnp.concatenate(best, axis=0)
        pr = jnp.exp(bestv - bestv[0:1])
        gate_ref[:, c * cw:(c + 1) * cw] = pr / jnp.sum(pr, axis=0, keepdims=True)
        idx_ref[:, c * cw:(c + 1) * cw] = jnp.concatenate(eid, axis=0).astype(I32) * rows_per_expert


def peer_topk(x, wq, sk, rows_per_expert, bt=256):
    T, D = x.shape
    nk, half = sk.shape[1], sk.shape[2]
    nh = wq.shape[1] // (2 * half)
    K = PEER_TOPK
    bt = _pick(T, bt)
    kern = functools.partial(_peer_topk_kernel, nk=nk, half=half, rows_per_expert=rows_per_expert)
    return pl.pallas_call(
        kern,
        grid=(T // bt, nh),
        in_specs=[pl.BlockSpec((bt, D), lambda i, h: (i, 0)),
                  pl.BlockSpec((D, 2 * half), lambda i, h: (0, h)),
                  pl.BlockSpec((2, nk, half), lambda i, h: (0, 0, 0))],
        out_specs=[pl.BlockSpec((K, bt), lambda i, h: (h, i)),
                   pl.BlockSpec((K, bt), lambda i, h: (h, i))],
        out_shape=[jax.ShapeDtypeStruct((nh * K, T), I32), jax.ShapeDtypeStruct((nh * K, T), F32)],
        compiler_params=_cparams(("arbitrary", "arbitrary")),
        name="peer_topk",
    )(x, wq, sk)


GROUP = 8


def _gather_rows(idx_ref, t0, tab_ref, tile_ref, nsel, p):
    rows = [idx_ref.at[t0 + i] for i in range(GROUP)]
    for k in range(nsel):
        for i in range(GROUP):
            e = pl.multiple_of(rows[i][k], p)
            tile_ref[pl.ds((i * nsel + k) * p, p), :] = tab_ref[pl.ds(e, p), :]


def _unpack_words(tile_ref, s, i, c, nsel, p):
    w = tile_ref[s, pl.ds(i * nsel * p + c, nsel, stride=p), :] if p > 1 else tile_ref[s, pl.ds(i * nsel, nsel), :]
    lo = lax.bitcast_convert_type(w << 16, F32)
    hi = lax.bitcast_convert_type(w & jnp.int32(-65536), F32)
    return lo, hi


def _peer_u_kernel(idx_ref, x_ref, gate_ref, tab_ref, act_ref, tile_ref, *, nsel, p):
    bt = x_ref.shape[0]
    lane = lax.broadcasted_iota(I32, (nsel, bt), 1)

    _gather_rows(idx_ref, 0, tab_ref, tile_ref.at[0], nsel, p)

    def body(j, acc):
        for s in range(2):
            t0 = (2 * j + s) * GROUP
            _gather_rows(idx_ref, jnp.minimum(t0 + GROUP, bt - GROUP), tab_ref, tile_ref.at[1 - s], nsel, p)
            for i in range(GROUP):
                t = t0 + i
                xrow = x_ref[pl.ds(t, 1), :]
                part = None
                for c in range(p):
                    lo, hi = _unpack_words(tile_ref, s, i, c, nsel, p)
                    term = (lo * xrow[:, c * LANES:(c + 1) * LANES]
                            + hi * xrow[:, (p + c) * LANES:(p + c + 1) * LANES])
                    part = term if part is None else part + term
                acc = jnp.where(lane == t, jnp.sum(part, axis=1, keepdims=True), acc)
        return acc

    hpre = lax.fori_loop(0, bt // (2 * GROUP), body, jnp.zeros((nsel, bt), F32))
    act_ref[...] = 0.5 * hpre * (1.0 + lax.erf(hpre * (2.0 ** -0.5))) * gate_ref[...]


def peer_u_phase(idx_t, x, gates, tab, p, bt=256):
    T, nsel = idx_t.shape
    D = x.shape[1]
    bt = _pick(T, bt)
    kern = functools.partial(_peer_u_kernel, nsel=nsel, p=p)
    return pl.pallas_call(
        kern,
        grid=(T // bt,),
        in_specs=[pl.BlockSpec((bt, nsel), lambda i: (i, 0), memory_space=pltpu.SMEM),
                  pl.BlockSpec((bt, D), lambda i: (i, 0)),
                  pl.BlockSpec((nsel, bt), lambda i: (0, i)),
                  pl.BlockSpec(tab.shape, lambda i: (0, 0), pipeline_mode=pl.Buffered(1))],
        out_specs=pl.BlockSpec((nsel, bt), lambda i: (0, i)),
        out_shape=jax.ShapeDtypeStruct((nsel, T), F32),
        scratch_shapes=[pltpu.VMEM((2, GROUP * nsel * p, LANES), I32)],
        compiler_params=_cparams(("arbitrary",), VMEM_TABLE_LIMIT),
        name="peer_u",
    )(idx_t, x, gates, tab)


def _peer_v_kernel(idx_ref, act_ref, x_ref, tab_ref, g_ref, b_ref, o_ref, tile_ref, out_ref, *, nsel, p, alpha):
    bt = x_ref.shape[0]
    lane = lax.broadcasted_iota(I32, (nsel, bt), 1)

    _gather_rows(idx_ref, 0, tab_ref, tile_ref.at[0], nsel, p)

    def body(j, carry):
        for s in range(2):
            t0 = (2 * j + s) * GROUP
            _gather_rows(idx_ref, jnp.minimum(t0 + GROUP, bt - GROUP), tab_ref, tile_ref.at[1 - s], nsel, p)
            for i in range(GROUP):
                t = t0 + i
                a = jnp.sum(jnp.where(lane == t, act_ref[...], 0.0), axis=1, keepdims=True)
                los, his = [], []
                for c in range(p):
                    lo, hi = _unpack_words(tile_ref, s, i, c, nsel, p)
                    los.append(jnp.sum(lo * a, axis=0, keepdims=True))
                    his.append(jnp.sum(hi * a, axis=0, keepdims=True))
                out_ref[pl.ds(t, 1), :] = jnp.concatenate(los + his, axis=1)
        return carry

    lax.fori_loop(0, bt // (2 * GROUP), body, 0)
    o_ref[...] = _layer_norm(alpha * x_ref[...] + out_ref[...], g_ref[...], b_ref[...])


def peer_v_phase(idx_t, act, x, tab, g, b, p, alpha, bt=256):
    T, nsel = idx_t.shape
    D = x.shape[1]
    bt = _pick(T, bt)
    kern = functools.partial(_peer_v_kernel, nsel=nsel, p=p, alpha=alpha)
    return pl.pallas_call(
        kern,
        grid=(T // bt,),
        in_specs=[pl.BlockSpec((bt, nsel), lambda i: (i, 0), memory_space=pltpu.SMEM),
                  pl.BlockSpec((nsel, bt), lambda i: (0, i)),
                  pl.BlockSpec((bt, D), lambda i: (i, 0)),
                  pl.BlockSpec(tab.shape, lambda i: (0, 0), pipeline_mode=pl.Buffered(1)),
                  pl.BlockSpec((1, D), lambda i: (0, 0)),
                  pl.BlockSpec((1, D), lambda i: (0, 0))],
        out_specs=pl.BlockSpec((bt, D), lambda i: (i, 0)),
        out_shape=jax.ShapeDtypeStruct((T, D), F32),
        scratch_shapes=[pltpu.VMEM((2, GROUP * nsel * p, LANES), I32), pltpu.VMEM((bt, D), F32)],
        compiler_params=_cparams(("arbitrary",), VMEM_TABLE_LIMIT),
        name="peer_v",
    )(idx_t, act, x, tab, g, b)


def pack_table(tab):
    E, D = tab.shape
    bits = lax.bitcast_convert_type(tab.astype(BF16), jnp.uint16).astype(jnp.uint32)
    words = bits[:, :D // 2] | (bits[:, D // 2:] << 16)
    return lax.bitcast_convert_type(words, I32).reshape(E * (D // 256), LANES)


def _kv_kernel(x_ref, wkv_ref, wf_ref, wft_ref, bf_ref, bft_ref, k_ref, v_ref, k16_ref, v16_ref, lf_ref, lft_ref, *, aw):
    xb = x_ref[...].astype(BF16)
    kv = jnp.dot(xb, wkv_ref[...], preferred_element_type=F32)
    k = kv[:, :aw]
    v = kv[:, aw:]
    k_ref[...] = k
    v_ref[...] = v
    k16_ref[...] = k.astype(BF16)
    v16_ref[...] = v.astype(BF16)
    f = jnp.dot(xb, wf_ref[...], preferred_element_type=F32) + bf_ref[...]
    lf_ref[...] = jax.nn.log_sigmoid(f)
    ft = lax.dot_general(wft_ref[...], xb, NT, preferred_element_type=F32) + bft_ref[...]
    lft_ref[...] = jax.nn.log_sigmoid(ft)


def shared_kv(x, wkv, wf, wft, bf, bft, bm=512):
    T, D = x.shape
    aw = wkv.shape[1] // 2
    H = wf.shape[1]
    bm = _pick(T, bm)
    outs = [jax.ShapeDtypeStruct((T, aw), F32), jax.ShapeDtypeStruct((T, aw), F32),
            jax.ShapeDtypeStruct((T, aw), BF16), jax.ShapeDtypeStruct((T, aw), BF16),
            jax.ShapeDtypeStruct((T, H), F32), jax.ShapeDtypeStruct((H, T), F32)]
    row = lambda i: (i, 0)
    fixed = lambda i: (0, 0)
    return pl.pallas_call(
        functools.partial(_kv_kernel, aw=aw),
        grid=(T // bm,),
        in_specs=[pl.BlockSpec((bm, D), row), pl.BlockSpec((D, 2 * aw), fixed), pl.BlockSpec((D, H), fixed),
                  pl.BlockSpec((H, D), fixed), pl.BlockSpec((1, H), fixed), pl.BlockSpec((H, 1), fixed)],
        out_specs=[pl.BlockSpec((bm, aw), row), pl.BlockSpec((bm, aw), row), pl.BlockSpec((bm, aw), row),
                   pl.BlockSpec((bm, aw), row), pl.BlockSpec((bm, H), row), pl.BlockSpec((H, bm), lambda i: (0, i))],
        out_shape=outs,
        compiler_params=_cparams(("arbitrary",)),
        name="shared_kv",
    )(x, wkv, wf, wft, bf, bft)


def _cumsum_kernel(lf_ref, lft_ref, ck_ref, ckt_ref):
    L = lf_ref.shape[1]
    B = LANES
    hi = lax.Precision.HIGHEST
    r = lax.broadcasted_iota(I32, (B, B), 0)
    c = lax.broadcasted_iota(I32, (B, B), 1)
    lower = (r >= c).astype(F32)
    upper = (r <= c).astype(F32)
    H = lf_ref.shape[2]
    carry = jnp.zeros((1, H), F32)
    carry_t = jnp.zeros((H, 1), F32)
    for j in range(L // B):
        blk = jnp.dot(lower, lf_ref[0, j * B:(j + 1) * B, :], precision=hi, preferred_element_type=F32) + carry
        ck_ref[0, j * B:(j + 1) * B, :] = blk
        carry = blk[B - 1:B, :]
        blk_t = jnp.dot(lft_ref[0, :, j * B:(j + 1) * B], upper, precision=hi, preferred_element_type=F32) + carry_t
        ckt_ref[0, :, j * B:(j + 1) * B] = blk_t
        carry_t = blk_t[:, B - 1:B]


def forget_cumsum(lf, lft):
    nseq, L, H = lf.shape
    return pl.pallas_call(
        _cumsum_kernel,
        grid=(nseq,),
        in_specs=[pl.BlockSpec((1, L, H), lambda s: (s, 0, 0)), pl.BlockSpec((1, H, L), lambda s: (s, 0, 0))],
        out_specs=[pl.BlockSpec((1, L, H), lambda s: (s, 0, 0)), pl.BlockSpec((1, H, L), lambda s: (s, 0, 0))],
        out_shape=[jax.ShapeDtypeStruct((nseq, L, H), F32), jax.ShapeDtypeStruct((nseq, H, L), F32)],
        compiler_params=_cparams(("arbitrary",)),
        name="forget_cumsum",
    )(lf, lft)


FOX_SUB_ROWS = 128


def _fox_kernel(q_ref, g_ref, k_ref, v_ref, cq_ref, ckt_ref, o_ref, *, bq, bk, dh, off, scale):
    hp = pl.program_id(1)
    qi = pl.program_id(2)
    rs = FOX_SUB_ROWS if bq % FOX_SUB_ROWS == 0 else bq
    nsub = bq // rs
    lane = lax.broadcasted_iota(I32, (rs, 2 * dh), 1)
    rowi = lax.broadcasted_iota(I32, (rs, bk), 0)
    kiota = lax.broadcasted_iota(I32, (rs, bk), 1)
    nkb = (off + (qi + 1) * bq + bk - 1) // bk
    head_lane = lax.broadcasted_iota(I32, (rs, cq_ref.shape[2]), 1)
    qh, cq, qpos = [], [], []
    for sb in range(nsub):
        q2 = q_ref[sb * rs:(sb + 1) * rs, :] * scale
        cq_all = cq_ref[0, sb * rs:(sb + 1) * rs, :]
        qpos.append(off + qi * bq + sb * rs + rowi)
        for hh in range(2):
            sel = (lane >= dh) if hh else (lane < dh)
            qh.append(jnp.where(sel, q2, 0.0).astype(BF16))
            cq.append(jnp.sum(jnp.where(head_lane == 2 * hp + hh, cq_all, 0.0), axis=1, keepdims=True))

    def body(kb, carry):
        ks = pl.multiple_of(kb * bk, bk)
        kblk = k_ref[0, pl.ds(ks, bk), :]
        vblk = v_ref[0, pl.ds(ks, bk), :]
        cks = [ckt_ref[0, 2 * hp + hh, :, pl.ds(ks, bk)] for hh in range(2)]
        nch = 2 * nsub
        scores = [lax.dot_general(qh[ch], kblk, NT, preferred_element_type=F32) for ch in range(nch)]
        stats, probs = [], []
        for ch in range(nch):
            m, l, _ = carry[ch]
            causal = kiota + ks <= qpos[ch // 2]
            s = jnp.where(causal, scores[ch] + cq[ch] - cks[ch % 2], -jnp.inf)
            m_new = jnp.maximum(m, jnp.max(s, axis=1, keepdims=True))
            alpha = jnp.exp(m - m_new)
            pexp = jnp.exp(s - m_new)
            stats.append((m_new, alpha * l + jnp.sum(pexp, axis=1, keepdims=True), alpha))
            probs.append(pexp.astype(BF16))
        new = []
        for ch in range(nch):
            m_new, l_new, alpha = stats[ch]
            acc = alpha * carry[ch][2] + jnp.dot(probs[ch], vblk, preferred_element_type=F32)
            new.append((m_new, l_new, acc))
        return tuple(new)

    init = (jnp.full((rs, 1), -jnp.inf, F32), jnp.zeros((rs, 1), F32), jnp.zeros((rs, 2 * dh), F32))
    fin = lax.fori_loop(0, nkb, body, (init,) * (2 * nsub))
    for sb in range(nsub):
        (_, l0, a0), (_, l1, a1) = fin[2 * sb], fin[2 * sb + 1]
        o = jnp.where(lane < dh, a0 / l0, a1 / l1)
        o_ref[sb * rs:(sb + 1) * rs, :] = o * jax.nn.sigmoid(g_ref[sb * rs:(sb + 1) * rs, :])


def fox_attention(qg, k16, v16, ck, ckt, tok0, nseq, L, Lk, dh, bq, bk):
    aw = k16.shape[2]
    Lkp = k16.shape[1]
    H = ck.shape[2]
    off = Lk - L
    nq = L // bq
    tb0 = tok0 // bq
    npair = aw // (2 * dh)
    kern = functools.partial(_fox_kernel, bq=bq, bk=bk, dh=dh, off=off, scale=dh ** -0.5)
    return pl.pallas_call(
        kern,
        grid=(nseq, npair, nq),
        in_specs=[pl.BlockSpec((bq, 2 * dh), lambda s, h, i: (tb0 + s * nq + i, h)),
                  pl.BlockSpec((bq, 2 * dh), lambda s, h, i: (tb0 + s * nq + i, npair + h)),
                  pl.BlockSpec((1, Lkp, 2 * dh), lambda s, h, i: (s, 0, h)),
                  pl.BlockSpec((1, Lkp, 2 * dh), lambda s, h, i: (s, 0, h)),
                  pl.BlockSpec((1, bq, H), lambda s, h, i: (s, off // bq + i, 0)),
                  pl.BlockSpec((1, H, 1, Lkp), lambda s, h, i: (s, 0, 0, 0))],
        out_specs=pl.BlockSpec((bq, 2 * dh), lambda s, h, i: (s * nq + i, h)),
        out_shape=jax.ShapeDtypeStruct((nseq * L, aw), F32),
        compiler_params=_cparams(("arbitrary", "arbitrary", "arbitrary")),
        name="fox_attention",
    )(qg, qg, k16, v16, ck, ckt.reshape(nseq, H, 1, Lkp))


def _pad_lanes(v, n=LANES):
    v = v.reshape(1, -1).astype(F32)
    return jnp.pad(v, ((0, 0), (0, n - v.shape[1])))


def _round_up(n, m):
    return (n + m - 1) // m * m


def kernel(x_prompt, x_sample, state_ssm, state_conv, cache_k, cache_v, cache_logf, a_w_in, a_conv_w, a_conv_b, a_dt_bias, a_A_log, a_D, a_norm_w, a_w_out, kv_w, kv_b_f, b_w_qg, b_w_o, peer_w_q, peer_subkeys, peer_u, peer_v, ln_g, ln_b):
    Bp, S, D = x_prompt.shape
    Bs, Ss, _ = x_sample.shape
    depth = ln_g.shape[0]
    n_a = a_w_in.shape[0]
    H = a_dt_bias.shape[1]
    DI = a_w_out.shape[1]
    P = DI // H
    N = state_ssm.shape[-1]
    C = a_conv_w.shape[2]
    G = (C - DI) // (2 * N)
    past = cache_k.shape[1]
    AH, dh = cache_k.shape[2], cache_k.shape[3]
    AW = AH * dh
    alpha = (2.0 * depth) ** 0.25
    Tp, Ts = Bp * S, Bs * Ss
    T = Tp + Ts
    nseq = Bp + Bs
    assert S % CHUNK == 0 and Ss % CHUNK == 0 and H <= LANES and D % 256 == 0

    seq_np = np.concatenate([np.repeat(np.arange(Bp), S // CHUNK), Bp + np.repeat(np.arange(Bs), Ss // CHUNK)])
    first_np = np.concatenate([[1], (seq_np[1:] != seq_np[:-1]).astype(np.int64)])
    seq = jnp.asarray(seq_np, I32)
    first = jnp.asarray(first_np, I32)

    x = jnp.concatenate([x_prompt.reshape(Tp, D), x_sample.reshape(Ts, D)], axis=0)
    rows_per_expert = D // 256
    new_conv, new_ssm = [], []
    k_f32 = v_f32 = lf = None

    for i in range(depth):
        g1, b1 = ln_g[i, 0].reshape(1, D), ln_b[i, 0].reshape(1, D)
        g2, b2 = ln_g[i, 1].reshape(1, D), ln_b[i, 1].reshape(1, D)
        if i < n_a:
            w_in = a_w_in[i].astype(BF16)
            z = matmul(x, w_in[:, :DI])
            xbc_raw = matmul(x, w_in[:, DI:DI + C])
            w_dt = jnp.pad(w_in[:, DI + C:], ((0, 0), (0, LANES - H)))
            dt_raw = matmul(x, w_dt)
            conv0 = jnp.concatenate([jnp.zeros((Bp,) + state_conv.shape[2:], F32), state_conv[i]], axis=0)
            xbc, conv_new = conv_silu(xbc_raw, conv0, a_conv_w[i], a_conv_b[i].reshape(1, C), first, seq)
            ssm0 = jnp.concatenate([jnp.zeros((Bp, H, P, N), F32), state_ssm[i]], axis=0)
            y, ssm_new = ssd_scan(xbc, dt_raw, _pad_lanes(a_dt_bias[i]), _pad_lanes(a_A_log[i]), _pad_lanes(a_D[i]),
                                  jnp.swapaxes(ssm0, 2, 3), first, seq, H, P, N, G)
            new_conv.append(conv_new)
            new_ssm.append(jnp.swapaxes(ssm_new, 2, 3))
            x = mamba_out(y, z, x, a_norm_w[i].reshape(1, DI), a_w_out[i].astype(BF16), g1, b1, G, alpha)
        else:
            j = i - n_a
            qg = matmul(x, b_w_qg[j].astype(BF16))
            o_p = fox_attention(qg, kp16, vp16, ck_p, ckt_p, 0, Bp, S, S, dh, bq_p, bk_p)
            o_s = fox_attention(qg, ks16, vs16, ck_s, ckt_s, Tp, Bs, Ss, past + Ss, dh, bq_s, bk_s)
            x = matmul_res_ln(jnp.concatenate([o_p, o_s], axis=0), b_w_o[j].astype(BF16), x, g1, b1, alpha)

        idx, gates = peer_topk(x, peer_w_q[i].astype(BF16), peer_subkeys[i].astype(BF16), rows_per_expert)
        idx_t = idx.T
        act = peer_u_phase(idx_t, x, gates, pack_table(peer_u[i]), rows_per_expert)
        x = peer_v_phase(idx_t, act, x, pack_table(peer_v[i]), g2, b2, rows_per_expert, alpha)

        if i == n_a - 1:
            wkv = kv_w[:, :2 * AW].astype(BF16)
            wf = kv_w[:, 2 * AW:].astype(BF16)
            k_f32, v_f32, k16, v16, lf, lft = shared_kv(x, wkv, wf, wf.T, kv_b_f.reshape(1, AH), kv_b_f.reshape(AH, 1))
            bq_p = _pick(S, 2 * FOX_SUB_ROWS)
            bk_p = _pick(S, 512)
            kp16 = k16[:Tp].reshape(Bp, S, AW)
            vp16 = v16[:Tp].reshape(Bp, S, AW)
            Sp = _round_up(S, LANES)
            lf_p = jnp.pad(lf[:Tp].reshape(Bp, S, AH), ((0, 0), (0, Sp - S), (0, 0)))
            lft_p = jnp.pad(jnp.swapaxes(lft[:, :Tp].reshape(AH, Bp, S), 0, 1), ((0, 0), (0, 0), (0, Sp - S)))
            ck_p, ckt_p = forget_cumsum(lf_p, lft_p)
            Lk = past + Ss
            bq_s = _pick(Ss, 128)
            Lkp = _round_up(Lk, LANES)
            bk_s = max(b for b in range(LANES, 4 * LANES + 1, LANES) if Lkp % b == 0)
            padk = ((0, 0), (0, Lkp - Lk), (0, 0))
            ks16 = jnp.pad(jnp.concatenate([cache_k.reshape(Bs, past, AW).astype(BF16), k16[Tp:].reshape(Bs, Ss, AW)], axis=1), padk)
            vs16 = jnp.pad(jnp.concatenate([cache_v.reshape(Bs, past, AW).astype(BF16), v16[Tp:].reshape(Bs, Ss, AW)], axis=1), padk)
            lf_s = jnp.pad(jnp.concatenate([cache_logf.astype(F32), lf[Tp:].reshape(Bs, Ss, AH)], axis=1), padk)
            lft_new = jnp.swapaxes(lft[:, Tp:].reshape(AH, Bs, Ss), 0, 1)
            lft_s = jnp.pad(jnp.concatenate([jnp.swapaxes(cache_logf.astype(F32), 1, 2), lft_new], axis=2),
                            ((0, 0), (0, 0), (0, Lkp - Lk)))
            ck_s, ckt_s = forget_cumsum(lf_s, lft_s)

    conv_all = jnp.stack(new_conv)
    ssm_all = jnp.stack(new_ssm)
    return (x[:Tp].reshape(Bp, S, D), x[Tp:].reshape(Bs, Ss, D),
            ssm_all[:, :Bp], conv_all[:, :Bp],
            k_f32[:Tp].reshape(Bp, S, AH, dh), v_f32[:Tp].reshape(Bp, S, AH, dh), lf[:Tp].reshape(Bp, S, AH),
            ssm_all[:, Bp:], conv_all[:, Bp:],
            k_f32[Tp:].reshape(Bs, Ss, AH, dh), v_f32[Tp:].reshape(Bs, Ss, AH, dh), lf[Tp:].reshape(Bs, Ss, AH))
```

```python
import functools
import math

import numpy as np
import jax
import jax.numpy as jnp
from jax import lax
from jax.experimental import pallas as pl
from jax.experimental.pallas import tpu as pltpu

F32 = jnp.float32
BF16 = jnp.bfloat16
I32 = jnp.int32

CHUNK = 64
PEER_TOPK = 16
RMS_EPS = 1e-5
LN_EPS = 1e-5
LANES = 128
VMEM_TABLE_LIMIT = 56 * 1024 * 1024
NT = (((1,), (1,)), ((), ()))


def _pick(n, pref):
    if n <= pref:
        return n
    for b in range(pref, 7, -1):
        if n % b == 0 and b % 8 == 0:
            return b
    return n


def _layer_norm(v, g, b):
    mu = jnp.mean(v, -1, keepdims=True)
    d = v - mu
    var = jnp.mean(d * d, -1, keepdims=True)
    return d * lax.rsqrt(var + LN_EPS) * g + b


def _cparams(sem, vmem=None):
    kw = dict(dimension_semantics=sem)
    if vmem is not None:
        kw["vmem_limit_bytes"] = vmem
    return pltpu.CompilerParams(**kw)


def _mm_kernel(x_ref, w_ref, o_ref):
    o_ref[...] = jnp.dot(x_ref[...].astype(BF16), w_ref[...], preferred_element_type=F32)


def matmul(x, w, bm=512, bn=1024):
    T, K = x.shape
    N = w.shape[1]
    bm = _pick(T, bm)
    bn = bn if N % bn == 0 else N
    return pl.pallas_call(
        _mm_kernel,
        grid=(N // bn, T // bm),
        in_specs=[pl.BlockSpec((bm, K), lambda j, i: (i, 0)),
                  pl.BlockSpec((K, bn), lambda j, i: (0, j))],
        out_specs=pl.BlockSpec((bm, bn), lambda j, i: (i, j)),
        out_shape=jax.ShapeDtypeStruct((T, N), F32),
        compiler_params=_cparams(("arbitrary", "arbitrary")),
        name="matmul",
    )(x, w)


def _mm_res_ln_kernel(a_ref, w_ref, x_ref, g_ref, b_ref, o_ref, *, alpha):
    y = jnp.dot(a_ref[...].astype(BF16), w_ref[...], preferred_element_type=F32)
    o_ref[...] = _layer_norm(alpha * x_ref[...] + y, g_ref[...], b_ref[...])


def matmul_res_ln(a, w, x, g, b, alpha, bm=512):
    T, K = a.shape
    D = w.shape[1]
    bm = _pick(T, bm)
    return pl.pallas_call(
        functools.partial(_mm_res_ln_kernel, alpha=alpha),
        grid=(T // bm,),
        in_specs=[pl.BlockSpec((bm, K), lambda i: (i, 0)),
                  pl.BlockSpec((K, D), lambda i: (0, 0)),
                  pl.BlockSpec((bm, D), lambda i: (i, 0)),
                  pl.BlockSpec((1, D), lambda i: (0, 0)),
                  pl.BlockSpec((1, D), lambda i: (0, 0))],
        out_specs=pl.BlockSpec((bm, D), lambda i: (i, 0)),
        out_shape=jax.ShapeDtypeStruct((T, D), F32),
        compiler_params=_cparams(("arbitrary",)),
        name="matmul_res_ln",
    )(a, w, x, g, b)


def _conv_kernel(first_ref, seq_ref, u_ref, init_ref, w_ref, b_ref, o_ref, st_ref, ext_ref, *, dconv):
    c = pl.program_id(0)
    nprev = dconv - 1

    @pl.when(first_ref[c] == 1)
    def _():
        ext_ref[8 - nprev:8, :] = init_ref[0]

    u = u_ref[...]
    ext_ref[8:8 + CHUNK, :] = u
    w = w_ref[...]
    acc = b_ref[...] + u * w[dconv - 1:dconv]
    for k in range(nprev):
        acc = acc + ext_ref[8 - nprev + k:8 - nprev + k + CHUNK, :] * w[k:k + 1]
    o_ref[...] = acc * jax.nn.sigmoid(acc)
    tail = ext_ref[8 + CHUNK - nprev:8 + CHUNK, :]
    st_ref[0] = tail
    ext_ref[8 - nprev:8, :] = tail


def conv_silu(u, init, w, b, first, seq):
    T, C = u.shape
    nseq, nprev, _ = init.shape
    dconv = nprev + 1
    grid_spec = pltpu.PrefetchScalarGridSpec(
        num_scalar_prefetch=2,
        grid=(T // CHUNK,),
        in_specs=[pl.BlockSpec((CHUNK, C), lambda c, f, s: (c, 0)),
                  pl.BlockSpec((1, nprev, C), lambda c, f, s: (s[c], 0, 0)),
                  pl.BlockSpec((dconv, C), lambda c, f, s: (0, 0)),
                  pl.BlockSpec((1, C), lambda c, f, s: (0, 0))],
        out_specs=[pl.BlockSpec((CHUNK, C), lambda c, f, s: (c, 0)),
                   pl.BlockSpec((1, nprev, C), lambda c, f, s: (s[c], 0, 0))],
        scratch_shapes=[pltpu.VMEM((8 + CHUNK, C), F32)],
    )
    return pl.pallas_call(
        functools.partial(_conv_kernel, dconv=dconv),
        grid_spec=grid_spec,
        out_shape=[jax.ShapeDtypeStruct((T, C), F32), jax.ShapeDtypeStruct((nseq, nprev, C), F32)],
        compiler_params=_cparams(("arbitrary",)),
        name="conv_silu",
    )(first, seq, u, init, w, b)


def _ssd_kernel(first_ref, seq_ref, xbc_ref, dt_ref, bias_ref, alog_ref, dsk_ref, init_ref,
                y_ref, fin_ref, st_ref, *, H, P, N, G):
    c = pl.program_id(0)

    @pl.when(first_ref[c] == 1)
    def _():
        st_ref[...] = init_ref[0]

    Q = CHUNK
    DI = H * P
    R = H // G
    per = LANES // P
    hi = lax.Precision.HIGHEST
    dt = jax.nn.softplus(dt_ref[...] + bias_ref[...])
    a = dt * (-jnp.exp(alog_ref[...]))
    row = lax.broadcasted_iota(I32, (Q, Q), 0)
    col = lax.broadcasted_iota(I32, (Q, Q), 1)
    tril = row >= col
    lane_q = lax.broadcasted_iota(I32, (Q, LANES), 1)
    a_cs = jnp.dot(tril.astype(F32), a, precision=hi, preferred_element_type=F32)
    eye_l = (lax.broadcasted_iota(I32, (LANES, LANES), 0) == lax.broadcasted_iota(I32, (LANES, LANES), 1)).astype(F32)
    a_cs_t = lax.dot_general(eye_l, a_cs, NT, precision=hi, preferred_element_type=F32)
    last = a_cs[Q - 1:Q, :]
    exp_cs = jnp.exp(a_cs)
    decay = jnp.exp(last - a_cs)
    chunk_decay = jnp.exp(last)
    dsk = dsk_ref[...]
    eye_n = (lax.broadcasted_iota(I32, (N, N), 0) == lax.broadcasted_iota(I32, (N, N), 1)).astype(BF16)

    for g in range(G):
        b16 = xbc_ref[:, DI + g * N:DI + (g + 1) * N].astype(BF16)
        c16 = xbc_ref[:, DI + (G + g) * N:DI + (G + g + 1) * N].astype(BF16)
        cb = lax.dot_general(c16, b16, NT, preferred_element_type=F32)
        b16_t = lax.dot_general(eye_n, b16, NT, preferred_element_type=F32).astype(BF16)
        slabs = [(g * R + r0) // per for r0 in range(0, R, per)]
        xs_l, xw16, st_l, ecs_l, cdec_l, xm16, m16_l = [], [], [], [], [], [], []
        for sl in slabs:
            h0 = sl * per
            xs_w = xbc_ref[:, sl * LANES:(sl + 1) * LANES]

            def wide(v, h0=h0):
                out = v[:, h0:h0 + 1]
                for hh in range(1, per):
                    out = jnp.where(lane_q >= hh * P, v[:, h0 + hh:h0 + hh + 1], out)
                return out

            xdt_w = xs_w * wide(dt)
            xs_l.append(xs_w)
            xw16.append((xdt_w * wide(decay)).astype(BF16))
            ecs_l.append(wide(exp_cs))
            cdec_l.append(wide(chunk_decay)[0:1])
            st_l.append(st_ref[sl])
            for hh in range(per):
                h = h0 + hh
                seg = a_cs[:, h:h + 1] - a_cs_t[h:h + 1, :]
                lmat = jnp.where(tril, jnp.exp(jnp.where(tril, seg, 0.0)), 0.0)
                m16_l.append((cb * lmat).astype(BF16))
                own = (lane_q >= hh * P) & (lane_q < (hh + 1) * P)
                xm16.append(jnp.where(own, xdt_w, 0.0).astype(BF16))
        ns = len(slabs)
        y_off = [jnp.dot(c16, st_l[i].astype(BF16), preferred_element_type=F32) for i in range(ns)]
        y_dia = [jnp.dot(m16_l[i], xm16[i], preferred_element_type=F32) for i in range(ns * per)]
        s_new = [jnp.dot(b16_t, xw16[i], preferred_element_type=F32) for i in range(ns)]
        for i, sl in enumerate(slabs):
            y = y_off[i] * ecs_l[i] + xs_l[i] * dsk[:, sl * LANES:(sl + 1) * LANES]
            for hh in range(per):
                y = y + y_dia[i * per + hh]
            y_ref[:, sl * LANES:(sl + 1) * LANES] = y
            st_ref[sl] = st_l[i] * cdec_l[i] + s_new[i]
    fin_ref[0] = st_ref[...]


def ssd_scan(xbc, dt_raw, dt_bias, a_log, d_skip, init_t, first, seq, H, P, N, G):
    T, C = xbc.shape
    nseq = init_t.shape[0]
    DI = H * P
    NS = DI // LANES
    kern = functools.partial(_ssd_kernel, H=H, P=P, N=N, G=G)
    grid_spec = pltpu.PrefetchScalarGridSpec(
        num_scalar_prefetch=2,
        grid=(T // CHUNK,),
        in_specs=[pl.BlockSpec((CHUNK, C), lambda c, f, s: (c, 0)),
                  pl.BlockSpec((CHUNK, LANES), lambda c, f, s: (c, 0)),
                  pl.BlockSpec((1, LANES), lambda c, f, s: (0, 0)),
                  pl.BlockSpec((1, LANES), lambda c, f, s: (0, 0)),
                  pl.BlockSpec((1, DI), lambda c, f, s: (0, 0)),
                  pl.BlockSpec((1, NS, N, LANES), lambda c, f, s: (s[c], 0, 0, 0))],
        out_specs=[pl.BlockSpec((CHUNK, DI), lambda c, f, s: (c, 0)),
                   pl.BlockSpec((1, NS, N, LANES), lambda c, f, s: (s[c], 0, 0, 0))],
        scratch_shapes=[pltpu.VMEM((NS, N, LANES), F32)],
    )
    return pl.pallas_call(
        kern,
        grid_spec=grid_spec,
        out_shape=[jax.ShapeDtypeStruct((T, DI), F32), jax.ShapeDtypeStruct((nseq, NS, N, LANES), F32)],
        compiler_params=_cparams(("arbitrary",)),
        name="ssd_scan",
    )(first, seq, xbc, dt_raw, dt_bias, a_log, d_skip, init_t)


def _mamba_out_kernel(y_ref, z_ref, x_ref, nw_ref, w_ref, g_ref, b_ref, o_ref, *, groups, alpha):
    z = z_ref[...]
    h = y_ref[...] * (z * jax.nn.sigmoid(z))
    gs = h.shape[1] // groups
    parts = []
    for gi in range(groups):
        hg = h[:, gi * gs:(gi + 1) * gs]
        parts.append(hg * lax.rsqrt(jnp.mean(hg * hg, -1, keepdims=True) + RMS_EPS))
    hn = jnp.concatenate(parts, axis=1) * nw_ref[...]
    out = jnp.dot(hn.astype(BF16), w_ref[...], preferred_element_type=F32)
    o_ref[...] = _layer_norm(alpha * x_ref[...] + out, g_ref[...], b_ref[...])


def mamba_out(y, z, x, norm_w, w_out, g, b, groups, alpha, bm=256):
    T, DI = y.shape
    D = x.shape[1]
    bm = _pick(T, bm)
    return pl.pallas_call(
        functools.partial(_mamba_out_kernel, groups=groups, alpha=alpha),
        grid=(T // bm,),
        in_specs=[pl.BlockSpec((bm, DI), lambda i: (i, 0)),
                  pl.BlockSpec((bm, DI), lambda i: (i, 0)),
                  pl.BlockSpec((bm, D), lambda i: (i, 0)),
                  pl.BlockSpec((1, DI), lambda i: (0, 0)),
                  pl.BlockSpec((DI, D), lambda i: (0, 0)),
                  pl.BlockSpec((1, D), lambda i: (0, 0)),
                  pl.BlockSpec((1, D), lambda i: (0, 0))],
        out_specs=pl.BlockSpec((bm, D), lambda i: (i, 0)),
        out_shape=jax.ShapeDtypeStruct((T, D), F32),
        compiler_params=_cparams(("arbitrary",)),
        name="mamba_out",
    )(y, z, x, norm_w, w_out, g, b)


def _peer_topk_kernel(x_ref, wq_ref, sk_ref, idx_ref, gate_ref, *, nk, half, rows_per_expert):
    q = jnp.dot(x_ref[...].astype(BF16), wq_ref[...], preferred_element_type=F32)
    bt = q.shape[0]
    K = PEER_TOPK
    cw = min(bt, LANES)
    neg = jnp.float32(-jnp.inf)
    iota_k = lax.broadcasted_iota(I32, (nk, cw), 0).astype(F32)

    def fiota(rows):
        return lax.broadcasted_iota(I32, (rows, cw), 0).astype(F32)

    for c in range(bt // cw):
        ss = []
        for j in range(2):
            qc = q[c * cw:(c + 1) * cw, j * half:(j + 1) * half].astype(BF16)
            ss.append(lax.dot_general(sk_ref[j], qc, NT, preferred_element_type=F32))
        v_rows, i_rows = ([], []), ([], [])
        for _ in range(K):
            for j in range(2):
                m = jnp.max(ss[j], axis=0, keepdims=True)
                am = jnp.min(jnp.where(ss[j] == m, iota_k, float(nk)), axis=0, keepdims=True)
                v_rows[j].append(m)
                i_rows[j].append(am)
                ss[j] = jnp.where(iota_k == am, neg, ss[j])
        sv = [jnp.concatenate(v_rows[j], axis=0) for j in range(2)]
        si = [jnp.concatenate(i_rows[j], axis=0) for j in range(2)]
        vals, ids, flats = [], [], []
        a = 0
        while a < K and K // (a + 1) >= 2:
            n_a = K // (a + 1)
            rows = _round_up(n_a, 8)
            b_pos = fiota(rows)
            vals.append(jnp.where(b_pos < n_a, sv[0][a:a + 1] + sv[1][:rows], neg))
            ids.append(si[0][a:a + 1] * float(nk) + si[1][:rows])
            flats.append(b_pos + float(a * K))
            a += 1
        if a < K:
            vals.append(sv[0][a:K] + sv[1][0:1])
            ids.append(si[0][a:K] * float(nk) + si[1][0:1])
            flats.append((fiota(K - a) + float(a)) * float(K))
        cand = jnp.concatenate(vals, axis=0)
        cid = jnp.concatenate(ids, axis=0)
        flat = jnp.concatenate(flats, axis=0)
        best, eid = [], []
        for _ in range(K):
            m = jnp.max(cand, axis=0, keepdims=True)
            pos = jnp.min(jnp.where(cand == m, flat, float(K * K)), axis=0, keepdims=True)
            hit = flat == pos
            eid.append(jnp.max(jnp.where(hit, cid, -1.0), axis=0, keepdims=True))
            best.append(m)
            cand = jnp.where(hit, neg, cand)
        bestv = jnp.concatenate(best, axis=0)
        pr = jnp.exp(bestv - bestv[0:1])
        gate_ref[:, c * cw:(c + 1) * cw] = pr / jnp.sum(pr, axis=0, keepdims=True)
        idx_ref[:, c * cw:(c + 1) * cw] = jnp.concatenate(eid, axis=0).astype(I32) * rows_per_expert


def peer_topk(x, wq, sk, rows_per_expert, bt=256):
    T, D = x.shape
    nk, half = sk.shape[1], sk.shape[2]
    nh = wq.shape[1] // (2 * half)
    K = PEER_TOPK
    bt = _pick(T, bt)
    kern = functools.partial(_peer_topk_kernel, nk=nk, half=half, rows_per_expert=rows_per_expert)
    return pl.pallas_call(
        kern,
        grid=(T // bt, nh),
        in_specs=[pl.BlockSpec((bt, D), lambda i, h: (i, 0)),
                  pl.BlockSpec((D, 2 * half), lambda i, h: (0, h)),
                  pl.BlockSpec((2, nk, half), lambda i, h: (0, 0, 0))],
        out_specs=[pl.BlockSpec((K, bt), lambda i, h: (h, i)),
                   pl.BlockSpec((K, bt), lambda i, h: (h, i))],
        out_shape=[jax.ShapeDtypeStruct((nh * K, T), I32), jax.ShapeDtypeStruct((nh * K, T), F32)],
        compiler_params=_cparams(("arbitrary", "arbitrary")),
        name="peer_topk",
    )(x, wq, sk)


GROUP = 8


def _gather_rows(idx_ref, t0, tab_ref, tile_ref, nsel, p):
    rows = [idx_ref.at[t0 + i] for i in range(GROUP)]
    for k in range(nsel):
        for i in range(GROUP):
            e = pl.multiple_of(rows[i][k], p)
            tile_ref[pl.ds((i * nsel + k) * p, p), :] = tab_ref[pl.ds(e, p), :]


def _unpack_words(tile_ref, s, i, c, nsel, p):
    w = tile_ref[s, pl.ds(i * nsel * p + c, nsel, stride=p), :] if p > 1 else tile_ref[s, pl.ds(i * nsel, nsel), :]
    lo = lax.bitcast_convert_type(w << 16, F32)
    hi = lax.bitcast_convert_type(w & jnp.int32(-65536), F32)
    return lo, hi


def _peer_u_kernel(idx_ref, x_ref, gate_ref, tab_ref, act_ref, tile_ref, *, nsel, p):
    bt = x_ref.shape[0]
    lane = lax.broadcasted_iota(I32, (nsel, bt), 1)

    _gather_rows(idx_ref, 0, tab_ref, tile_ref.at[0], nsel, p)

    def body(j, acc):
        for s in range(2):
            t0 = (2 * j + s) * GROUP
            _gather_rows(idx_ref, jnp.minimum(t0 + GROUP, bt - GROUP), tab_ref, tile_ref.at[1 - s], nsel, p)
            for i in range(GROUP):
                t = t0 + i
                xrow = x_ref[pl.ds(t, 1), :]
                part = None
                for c in range(p):
                    lo, hi = _unpack_words(tile_ref, s, i, c, nsel, p)
                    term = (lo * xrow[:, c * LANES:(c + 1) * LANES]
                            + hi * xrow[:, (p + c) * LANES:(p + c + 1) * LANES])
                    part = term if part is None else part + term
                acc = jnp.where(lane == t, jnp.sum(part, axis=1, keepdims=True), acc)
        return acc

    hpre = lax.fori_loop(0, bt // (2 * GROUP), body, jnp.zeros((nsel, bt), F32))
    act_ref[...] = 0.5 * hpre * (1.0 + lax.erf(hpre * (2.0 ** -0.5))) * gate_ref[...]


def peer_u_phase(idx_t, x, gates, tab, p, bt=256):
    T, nsel = idx_t.shape
    D = x.shape[1]
    bt = _pick(T, bt)
    kern = functools.partial(_peer_u_kernel, nsel=nsel, p=p)
    return pl.pallas_call(
        kern,
        grid=(T // bt,),
        in_specs=[pl.BlockSpec((bt, nsel), lambda i: (i, 0), memory_space=pltpu.SMEM),
                  pl.BlockSpec((bt, D), lambda i: (i, 0)),
                  pl.BlockSpec((nsel, bt), lambda i: (0, i)),
                  pl.BlockSpec(tab.shape, lambda i: (0, 0), pipeline_mode=pl.Buffered(1))],
        out_specs=pl.BlockSpec((nsel, bt), lambda i: (0, i)),
        out_shape=jax.ShapeDtypeStruct((nsel, T), F32),
        scratch_shapes=[pltpu.VMEM((2, GROUP * nsel * p, LANES), I32)],
        compiler_params=_cparams(("arbitrary",), VMEM_TABLE_LIMIT),
        name="peer_u",
    )(idx_t, x, gates, tab)


def _peer_v_kernel(idx_ref, act_ref, x_ref, tab_ref, g_ref, b_ref, o_ref, tile_ref, out_ref, *, nsel, p, alpha):
    bt = x_ref.shape[0]
    lane = lax.broadcasted_iota(I32, (nsel, bt), 1)

    _gather_rows(idx_ref, 0, tab_ref, tile_ref.at[0], nsel, p)

    def body(j, carry):
        for s in range(2):
            t0 = (2 * j + s) * GROUP
            _gather_rows(idx_ref, jnp.minimum(t0 + GROUP, bt - GROUP), tab_ref, tile_ref.at[1 - s], nsel, p)
            for i in range(GROUP):
                t = t0 + i
                a = jnp.sum(jnp.where(lane == t, act_ref[...], 0.0), axis=1, keepdims=True)
                los, his = [], []
                for c in range(p):
                    lo, hi = _unpack_words(tile_ref, s, i, c, nsel, p)
                    los.append(jnp.sum(lo * a, axis=0, keepdims=True))
                    his.append(jnp.sum(hi * a, axis=0, keepdims=True))
                out_ref[pl.ds(t, 1), :] = jnp.concatenate(los + his, axis=1)
        return carry

    lax.fori_loop(0, bt // (2 * GROUP), body, 0)
    o_ref[...] = _layer_norm(alpha * x_ref[...] + out_ref[...], g_ref[...], b_ref[...])


def peer_v_phase(idx_t, act, x, tab, g, b, p, alpha, bt=256):
    T, nsel = idx_t.shape
    D = x.shape[1]
    bt = _pick(T, bt)
    kern = functools.partial(_peer_v_kernel, nsel=nsel, p=p, alpha=alpha)
    return pl.pallas_call(
        kern,
        grid=(T // bt,),
        in_specs=[pl.BlockSpec((bt, nsel), lambda i: (i, 0), memory_space=pltpu.SMEM),
                  pl.BlockSpec((nsel, bt), lambda i: (0, i)),
                  pl.BlockSpec((bt, D), lambda i: (i, 0)),
                  pl.BlockSpec(tab.shape, lambda i: (0, 0), pipeline_mode=pl.Buffered(1)),
                  pl.BlockSpec((1, D), lambda i: (0, 0)),
                  pl.BlockSpec((1, D), lambda i: (0, 0))],
        out_specs=pl.BlockSpec((bt, D), lambda i: (i, 0)),
        out_shape=jax.ShapeDtypeStruct((T, D), F32),
        scratch_shapes=[pltpu.VMEM((2, GROUP * nsel * p, LANES), I32), pltpu.VMEM((bt, D), F32)],
        compiler_params=_cparams(("arbitrary",), VMEM_TABLE_LIMIT),
        name="peer_v",
    )(idx_t, act, x, tab, g, b)


def pack_table(tab):
    E, D = tab.shape
    bits = lax.bitcast_convert_type(tab.astype(BF16), jnp.uint16).astype(jnp.uint32)
    words = bits[:, :D // 2] | (bits[:, D // 2:] << 16)
    return lax.bitcast_convert_type(words, I32).reshape(E * (D // 256), LANES)


def _kv_kernel(x_ref, wkv_ref, wf_ref, wft_ref, bf_ref, bft_ref, k_ref, v_ref, k16_ref, v16_ref, lf_ref, lft_ref, *, aw):
    xb = x_ref[...].astype(BF16)
    kv = jnp.dot(xb, wkv_ref[...], preferred_element_type=F32)
    k = kv[:, :aw]
    v = kv[:, aw:]
    k_ref[...] = k
    v_ref[...] = v
    k16_ref[...] = k.astype(BF16)
    v16_ref[...] = v.astype(BF16)
    f = jnp.dot(xb, wf_ref[...], preferred_element_type=F32) + bf_ref[...]
    lf_ref[...] = jax.nn.log_sigmoid(f)
    ft = lax.dot_general(wft_ref[...], xb, NT, preferred_element_type=F32) + bft_ref[...]
    lft_ref[...] = jax.nn.log_sigmoid(ft)


def shared_kv(x, wkv, wf, wft, bf, bft, bm=512):
    T, D = x.shape
    aw = wkv.shape[1] // 2
    H = wf.shape[1]
    bm = _pick(T, bm)
    outs = [jax.ShapeDtypeStruct((T, aw), F32), jax.ShapeDtypeStruct((T, aw), F32),
            jax.ShapeDtypeStruct((T, aw), BF16), jax.ShapeDtypeStruct((T, aw), BF16),
            jax.ShapeDtypeStruct((T, H), F32), jax.ShapeDtypeStruct((H, T), F32)]
    row = lambda i: (i, 0)
    fixed = lambda i: (0, 0)
    return pl.pallas_call(
        functools.partial(_kv_kernel, aw=aw),
        grid=(T // bm,),
        in_specs=[pl.BlockSpec((bm, D), row), pl.BlockSpec((D, 2 * aw), fixed), pl.BlockSpec((D, H), fixed),
                  pl.BlockSpec((H, D), fixed), pl.BlockSpec((1, H), fixed), pl.BlockSpec((H, 1), fixed)],
        out_specs=[pl.BlockSpec((bm, aw), row), pl.BlockSpec((bm, aw), row), pl.BlockSpec((bm, aw), row),
                   pl.BlockSpec((bm, aw), row), pl.BlockSpec((bm, H), row), pl.BlockSpec((H, bm), lambda i: (0, i))],
        out_shape=outs,
        compiler_params=_cparams(("arbitrary",)),
        name="shared_kv",
    )(x, wkv, wf, wft, bf, bft)


def _cumsum_kernel(lf_ref, lft_ref, ck_ref, ckt_ref):
    L = lf_ref.shape[1]
    B = LANES
    hi = lax.Precision.HIGHEST
    r = lax.broadcasted_iota(I32, (B, B), 0)
    c = lax.broadcasted_iota(I32, (B, B), 1)
    lower = (r >= c).astype(F32)
    upper = (r <= c).astype(F32)
    H = lf_ref.shape[2]
    carry = jnp.zeros((1, H), F32)
    carry_t = jnp.zeros((H, 1), F32)
    for j in range(L // B):
        blk = jnp.dot(lower, lf_ref[0, j * B:(j + 1) * B, :], precision=hi, preferred_element_type=F32) + carry
        ck_ref[0, j * B:(j + 1) * B, :] = blk
        carry = blk[B - 1:B, :]
        blk_t = jnp.dot(lft_ref[0, :, j * B:(j + 1) * B], upper, precision=hi, preferred_element_type=F32) + carry_t
        ckt_ref[0, :, j * B:(j + 1) * B] = blk_t
        carry_t = blk_t[:, B - 1:B]


def forget_cumsum(lf, lft):
    nseq, L, H = lf.shape
    return pl.pallas_call(
        _cumsum_kernel,
        grid=(nseq,),
        in_specs=[pl.BlockSpec((1, L, H), lambda s: (s, 0, 0)), pl.BlockSpec((1, H, L), lambda s: (s, 0, 0))],
        out_specs=[pl.BlockSpec((1, L, H), lambda s: (s, 0, 0)), pl.BlockSpec((1, H, L), lambda s: (s, 0, 0))],
        out_shape=[jax.ShapeDtypeStruct((nseq, L, H), F32), jax.ShapeDtypeStruct((nseq, H, L), F32)],
        compiler_params=_cparams(("arbitrary",)),
        name="forget_cumsum",
    )(lf, lft)


FOX_SUB_ROWS = 128


def _fox_kernel(q_ref, g_ref, k_ref, v_ref, cq_ref, ckt_ref, o_ref, *, bq, bk, dh, off, scale):
    hp = pl.program_id(1)
    qi = pl.program_id(2)
    rs = FOX_SUB_ROWS if bq % FOX_SUB_ROWS == 0 else bq
    nsub = bq // rs
    lane = lax.broadcasted_iota(I32, (rs, 2 * dh), 1)
    rowi = lax.broadcasted_iota(I32, (rs, bk), 0)
    kiota = lax.broadcasted_iota(I32, (rs, bk), 1)
    nkb = (off + (qi + 1) * bq + bk - 1) // bk
    head_lane = lax.broadcasted_iota(I32, (rs, cq_ref.shape[2]), 1)
    qh, cq, qpos = [], [], []
    for sb in range(nsub):
        q2 = q_ref[sb * rs:(sb + 1) * rs, :] * scale
        cq_all = cq_ref[0, sb * rs:(sb + 1) * rs, :]
        qpos.append(off + qi * bq + sb * rs + rowi)
        for hh in range(2):
            sel = (lane >= dh) if hh else (lane < dh)
            qh.append(jnp.where(sel, q2, 0.0).astype(BF16))
            cq.append(jnp.sum(jnp.where(head_lane == 2 * hp + hh, cq_all, 0.0), axis=1, keepdims=True))

    def body(kb, carry):
        ks = pl.multiple_of(kb * bk, bk)
        kblk = k_ref[0, pl.ds(ks, bk), :]
        vblk = v_ref[0, pl.ds(ks, bk), :]
        cks = [ckt_ref[0, 2 * hp + hh, :, pl.ds(ks, bk)] for hh in range(2)]
        nch = 2 * nsub
        scores = [lax.dot_general(qh[ch], kblk, NT, preferred_element_type=F32) for ch in range(nch)]
        stats, probs = [], []
        for ch in range(nch):
            m, l, _ = carry[ch]
            causal = kiota + ks <= qpos[ch // 2]
            s = jnp.where(causal, scores[ch] + cq[ch] - cks[ch % 2], -jnp.inf)
            m_new = jnp.maximum(m, jnp.max(s, axis=1, keepdims=True))
            alpha = jnp.exp(m - m_new)
            pexp = jnp.exp(s - m_new)
            stats.append((m_new, alpha * l + jnp.sum(pexp, axis=1, keepdims=True), alpha))
            probs.append(pexp.astype(BF16))
        new = []
        for ch in range(nch):
            m_new, l_new, alpha = stats[ch]
            acc = alpha * carry[ch][2] + jnp.dot(probs[ch], vblk, preferred_element_type=F32)
            new.append((m_new, l_new, acc))
        return tuple(new)

    init = (jnp.full((rs, 1), -jnp.inf, F32), jnp.zeros((rs, 1), F32), jnp.zeros((rs, 2 * dh), F32))
    fin = lax.fori_loop(0, nkb, body, (init,) * (2 * nsub))
    for sb in range(nsub):
        (_, l0, a0), (_, l1, a1) = fin[2 * sb], fin[2 * sb + 1]
        o = jnp.where(lane < dh, a0 / l0, a1 / l1)
        o_ref[sb * rs:(sb + 1) * rs, :] = o * jax.nn.sigmoid(g_ref[sb * rs:(sb + 1) * rs, :])


def fox_attention(qg, k16, v16, ck, ckt, tok0, nseq, L, Lk, dh, bq, bk):
    aw = k16.shape[2]
    Lkp = k16.shape[1]
    H = ck.shape[2]
    off = Lk - L
    nq = L // bq
    tb0 = tok0 // bq
    npair = aw // (2 * dh)
    kern = functools.partial(_fox_kernel, bq=bq, bk=bk, dh=dh, off=off, scale=dh ** -0.5)
    return pl.pallas_call(
        kern,
        grid=(nseq, npair, nq),
        in_specs=[pl.BlockSpec((bq, 2 * dh), lambda s, h, i: (tb0 + s * nq + i, h)),
                  pl.BlockSpec((bq, 2 * dh), lambda s, h, i: (tb0 + s * nq + i, npair + h)),
                  pl.BlockSpec((1, Lkp, 2 * dh), lambda s, h, i: (s, 0, h)),
                  pl.BlockSpec((1, Lkp, 2 * dh), lambda s, h, i: (s, 0, h)),
                  pl.BlockSpec((1, bq, H), lambda s, h, i: (s, off // bq + i, 0)),
                  pl.BlockSpec((1, H, 1, Lkp), lambda s, h, i: (s, 0, 0, 0))],
        out_specs=pl.BlockSpec((bq, 2 * dh), lambda s, h, i: (s * nq + i, h)),
        out_shape=jax.ShapeDtypeStruct((nseq * L, aw), F32),
        compiler_params=_cparams(("arbitrary", "arbitrary", "arbitrary")),
        name="fox_attention",
    )(qg, qg, k16, v16, ck, ckt.reshape(nseq, H, 1, Lkp))


def _pad_lanes(v, n=LANES):
    v = v.reshape(1, -1).astype(F32)
    return jnp.pad(v, ((0, 0), (0, n - v.shape[1])))


def _round_up(n, m):
    return (n + m - 1) // m * m


def kernel(x_prompt, x_sample, state_ssm, state_conv, cache_k, cache_v, cache_logf, a_w_in, a_conv_w, a_conv_b, a_dt_bias, a_A_log, a_D, a_norm_w, a_w_out, kv_w, kv_b_f, b_w_qg, b_w_o, peer_w_q, peer_subkeys, peer_u, peer_v, ln_g, ln_b):
    Bp, S, D = x_prompt.shape
    Bs, Ss, _ = x_sample.shape
    depth = ln_g.shape[0]
    n_a = a_w_in.shape[0]
    H = a_dt_bias.shape[1]
    DI = a_w_out.shape[1]
    P = DI // H
    N = state_ssm.shape[-1]
    C = a_conv_w.shape[2]
    G = (C - DI) // (2 * N)
    past = cache_k.shape[1]
    AH, dh = cache_k.shape[2], cache_k.shape[3]
    AW = AH * dh
    alpha = (2.0 * depth) ** 0.25
    Tp, Ts = Bp * S, Bs * Ss
    T = Tp + Ts
    nseq = Bp + Bs
    assert S % CHUNK == 0 and Ss % CHUNK == 0 and H <= LANES and D % 256 == 0 and LANES % P == 0
    per = LANES // P
    assert (H // G) % per == 0

    seq_np = np.concatenate([np.repeat(np.arange(Bp), S // CHUNK), Bp + np.repeat(np.arange(Bs), Ss // CHUNK)])
    first_np = np.concatenate([[1], (seq_np[1:] != seq_np[:-1]).astype(np.int64)])
    seq = jnp.asarray(seq_np, I32)
    first = jnp.asarray(first_np, I32)

    x = jnp.concatenate([x_prompt.reshape(Tp, D), x_sample.reshape(Ts, D)], axis=0)
    rows_per_expert = D // 256
    new_conv, new_ssm = [], []
    k_f32 = v_f32 = lf = None

    for i in range(depth):
        g1, b1 = ln_g[i, 0].reshape(1, D), ln_b[i, 0].reshape(1, D)
        g2, b2 = ln_g[i, 1].reshape(1, D), ln_b[i, 1].reshape(1, D)
        if i < n_a:
            w_in = a_w_in[i].astype(BF16)
            z = matmul(x, w_in[:, :DI])
            xbc_raw = matmul(x, w_in[:, DI:DI + C])
            w_dt = jnp.pad(w_in[:, DI + C:], ((0, 0), (0, LANES - H)))
            dt_raw = matmul(x, w_dt)
            conv0 = jnp.concatenate([jnp.zeros((Bp,) + state_conv.shape[2:], F32), state_conv[i]], axis=0)
            xbc, conv_new = conv_silu(xbc_raw, conv0, a_conv_w[i], a_conv_b[i].reshape(1, C), first, seq)
            ssm0 = jnp.concatenate([jnp.zeros((Bp, H, P, N), F32), state_ssm[i]], axis=0)
            ssm0_t = ssm0.reshape(nseq, H // per, per, P, N).transpose(0, 1, 4, 2, 3).reshape(nseq, H // per, N, LANES)
            y, ssm_new = ssd_scan(xbc, dt_raw, _pad_lanes(a_dt_bias[i]), _pad_lanes(a_A_log[i]),
                                  jnp.repeat(a_D[i].astype(F32), P).reshape(1, DI), ssm0_t, first, seq, H, P, N, G)
            new_conv.append(conv_new)
            new_ssm.append(ssm_new.reshape(nseq, H // per, N, per, P).transpose(0, 1, 3, 4, 2).reshape(nseq, H, P, N))
            x = mamba_out(y, z, x, a_norm_w[i].reshape(1, DI), a_w_out[i].astype(BF16), g1, b1, G, alpha)
        else:
            j = i - n_a
            qg = matmul(x, b_w_qg[j].astype(BF16))
            o_p = fox_attention(qg, kp16, vp16, ck_p, ckt_p, 0, Bp, S, S, dh, bq_p, bk_p)
            o_s = fox_attention(qg, ks16, vs16, ck_s, ckt_s, Tp, Bs, Ss, past + Ss, dh, bq_s, bk_s)
            x = matmul_res_ln(jnp.concatenate([o_p, o_s], axis=0), b_w_o[j].astype(BF16), x, g1, b1, alpha)

        idx, gates = peer_topk(x, peer_w_q[i].astype(BF16), peer_subkeys[i].astype(BF16), rows_per_expert)
        idx_t = idx.T
        act = peer_u_phase(idx_t, x, gates, pack_table(peer_u[i]), rows_per_expert)
        x = peer_v_phase(idx_t, act, x, pack_table(peer_v[i]), g2, b2, rows_per_expert, alpha)

        if i == n_a - 1:
            wkv = kv_w[:, :2 * AW].astype(BF16)
            wf = kv_w[:, 2 * AW:].astype(BF16)
            k_f32, v_f32, k16, v16, lf, lft = shared_kv(x, wkv, wf, wf.T, kv_b_f.reshape(1, AH), kv_b_f.reshape(AH, 1))
            bq_p = _pick(S, 2 * FOX_SUB_ROWS)
            bk_p = _pick(S, 512)
            kp16 = k16[:Tp].reshape(Bp, S, AW)
            vp16 = v16[:Tp].reshape(Bp, S, AW)
            Sp = _round_up(S, LANES)
            lf_p = jnp.pad(lf[:Tp].reshape(Bp, S, AH), ((0, 0), (0, Sp - S), (0, 0)))
            lft_p = jnp.pad(jnp.swapaxes(lft[:, :Tp].reshape(AH, Bp, S), 0, 1), ((0, 0), (0, 0), (0, Sp - S)))
            ck_p, ckt_p = forget_cumsum(lf_p, lft_p)
            Lk = past + Ss
            bq_s = _pick(Ss, 128)
            Lkp = _round_up(Lk, LANES)
            bk_s = max(b for b in range(LANES, 4 * LANES + 1, LANES) if Lkp % b == 0)
            padk = ((0, 0), (0, Lkp - Lk), (0, 0))
            ks16 = jnp.pad(jnp.concatenate([cache_k.reshape(Bs, past, AW).astype(BF16), k16[Tp:].reshape(Bs, Ss, AW)], axis=1), padk)
            vs16 = jnp.pad(jnp.concatenate([cache_v.reshape(Bs, past, AW).astype(BF16), v16[Tp:].reshape(Bs, Ss, AW)], axis=1), padk)
            lf_s = jnp.pad(jnp.concatenate([cache_logf.astype(F32), lf[Tp:].reshape(Bs, Ss, AH)], axis=1), padk)
            lft_new = jnp.swapaxes(lft[:, Tp:].reshape(AH, Bs, Ss), 0, 1)
            lft_s = jnp.pad(jnp.concatenate([jnp.swapaxes(cache_logf.astype(F32), 1, 2), lft_new], axis=2),
                            ((0, 0), (0, 0), (0, Lkp - Lk)))
            ck_s, ckt_s = forget_cumsum(lf_s, lft_s)

    conv_all = jnp.stack(new_conv)
    ssm_all = jnp.stack(new_ssm)
    return (x[:Tp].reshape(Bp, S, D), x[Tp:].reshape(Bs, Ss, D),
            ssm_all[:, :Bp], conv_all[:, :Bp],
            k_f32[:Tp].reshape(Bp, S, AH, dh), v_f32[:Tp].reshape(Bp, S, AH, dh), lf[:Tp].reshape(Bp, S, AH),
            ssm_all[:, Bp:], conv_all[:, Bp:],
            k_f32[Tp:].reshape(Bs, Ss, AH, dh), v_f32[Tp:].reshape(Bs, Ss, AH, dh), lf[Tp:].reshape(Bs, Ss, AH))
```

```python
import functools
import math

import numpy as np
import jax
import jax.numpy as jnp
from jax import lax
from jax.experimental import pallas as pl
from jax.experimental.pallas import tpu as pltpu

F32 = jnp.float32
BF16 = jnp.bfloat16
I32 = jnp.int32

CHUNK = 64
PEER_TOPK = 16
RMS_EPS = 1e-5
LN_EPS = 1e-5
LANES = 128
VMEM_TABLE_LIMIT = 56 * 1024 * 1024
NT = (((1,), (1,)), ((), ()))


def _pick(n, pref):
    if n <= pref:
        return n
    for b in range(pref, 7, -1):
        if n % b == 0 and b % 8 == 0:
            return b
    return n


def _layer_norm(v, g, b):
    mu = jnp.mean(v, -1, keepdims=True)
    d = v - mu
    var = jnp.mean(d * d, -1, keepdims=True)
    return d * lax.rsqrt(var + LN_EPS) * g + b


def _cparams(sem, vmem=None):
    kw = dict(dimension_semantics=sem)
    if vmem is not None:
        kw["vmem_limit_bytes"] = vmem
    return pltpu.CompilerParams(**kw)


def _mm_kernel(x_ref, w_ref, o_ref):
    o_ref[...] = jnp.dot(x_ref[...].astype(BF16), w_ref[...], preferred_element_type=F32)


def matmul(x, w, bm=512, bn=1024):
    T, K = x.shape
    N = w.shape[1]
    bm = _pick(T, bm)
    bn = bn if N % bn == 0 else N
    return pl.pallas_call(
        _mm_kernel,
        grid=(N // bn, T // bm),
        in_specs=[pl.BlockSpec((bm, K), lambda j, i: (i, 0)),
                  pl.BlockSpec((K, bn), lambda j, i: (0, j))],
        out_specs=pl.BlockSpec((bm, bn), lambda j, i: (i, j)),
        out_shape=jax.ShapeDtypeStruct((T, N), F32),
        compiler_params=_cparams(("arbitrary", "arbitrary")),
        name="matmul",
    )(x, w)


def _mm_res_ln_kernel(a_ref, w_ref, x_ref, g_ref, b_ref, o_ref, *, alpha):
    y = jnp.dot(a_ref[...].astype(BF16), w_ref[...], preferred_element_type=F32)
    o_ref[...] = _layer_norm(alpha * x_ref[...] + y, g_ref[...], b_ref[...])


def matmul_res_ln(a, w, x, g, b, alpha, bm=512):
    T, K = a.shape
    D = w.shape[1]
    bm = _pick(T, bm)
    return pl.pallas_call(
        functools.partial(_mm_res_ln_kernel, alpha=alpha),
        grid=(T // bm,),
        in_specs=[pl.BlockSpec((bm, K), lambda i: (i, 0)),
                  pl.BlockSpec((K, D), lambda i: (0, 0)),
                  pl.BlockSpec((bm, D), lambda i: (i, 0)),
                  pl.BlockSpec((1, D), lambda i: (0, 0)),
                  pl.BlockSpec((1, D), lambda i: (0, 0))],
        out_specs=pl.BlockSpec((bm, D), lambda i: (i, 0)),
        out_shape=jax.ShapeDtypeStruct((T, D), F32),
        compiler_params=_cparams(("arbitrary",)),
        name="matmul_res_ln",
    )(a, w, x, g, b)


def _conv_kernel(first_ref, seq_ref, u_ref, init_ref, w_ref, b_ref, o_ref, st_ref, ext_ref, *, dconv):
    c = pl.program_id(0)
    nprev = dconv - 1

    @pl.when(first_ref[c] == 1)
    def _():
        ext_ref[8 - nprev:8, :] = init_ref[0]

    u = u_ref[...]
    ext_ref[8:8 + CHUNK, :] = u
    w = w_ref[...]
    acc = b_ref[...] + u * w[dconv - 1:dconv]
    for k in range(nprev):
        acc = acc + ext_ref[8 - nprev + k:8 - nprev + k + CHUNK, :] * w[k:k + 1]
    o_ref[...] = acc * jax.nn.sigmoid(acc)
    tail = ext_ref[8 + CHUNK - nprev:8 + CHUNK, :]
    st_ref[0] = tail
    ext_ref[8 - nprev:8, :] = tail


def conv_silu(u, init, w, b, first, seq):
    T, C = u.shape
    nseq, nprev, _ = init.shape
    dconv = nprev + 1
    grid_spec = pltpu.PrefetchScalarGridSpec(
        num_scalar_prefetch=2,
        grid=(T // CHUNK,),
        in_specs=[pl.BlockSpec((CHUNK, C), lambda c, f, s: (c, 0)),
                  pl.BlockSpec((1, nprev, C), lambda c, f, s: (s[c], 0, 0)),
                  pl.BlockSpec((dconv, C), lambda c, f, s: (0, 0)),
                  pl.BlockSpec((1, C), lambda c, f, s: (0, 0))],
        out_specs=[pl.BlockSpec((CHUNK, C), lambda c, f, s: (c, 0)),
                   pl.BlockSpec((1, nprev, C), lambda c, f, s: (s[c], 0, 0))],
        scratch_shapes=[pltpu.VMEM((8 + CHUNK, C), F32)],
    )
    return pl.pallas_call(
        functools.partial(_conv_kernel, dconv=dconv),
        grid_spec=grid_spec,
        out_shape=[jax.ShapeDtypeStruct((T, C), F32), jax.ShapeDtypeStruct((nseq, nprev, C), F32)],
        compiler_params=_cparams(("arbitrary",)),
        name="conv_silu",
    )(first, seq, u, init, w, b)


def _ssd_kernel(first_ref, seq_ref, xbc_ref, dt_ref, bias_ref, alog_ref, dsk_ref, init_ref,
                y_ref, fin_ref, st_ref, *, H, P, N, G):
    c = pl.program_id(0)

    @pl.when(first_ref[c] == 1)
    def _():
        st_ref[...] = init_ref[0]

    Q = CHUNK
    DI = H * P
    R = H // G
    per = LANES // P
    hi = lax.Precision.HIGHEST
    dt = jax.nn.softplus(dt_ref[...] + bias_ref[...])
    a = dt * (-jnp.exp(alog_ref[...]))
    row = lax.broadcasted_iota(I32, (Q, Q), 0)
    col = lax.broadcasted_iota(I32, (Q, Q), 1)
    tril = row >= col
    lane_q = lax.broadcasted_iota(I32, (Q, LANES), 1)
    a_cs = jnp.dot(tril.astype(F32), a, precision=hi, preferred_element_type=F32)
    eye_l = (lax.broadcasted_iota(I32, (LANES, LANES), 0) == lax.broadcasted_iota(I32, (LANES, LANES), 1)).astype(F32)
    a_cs_t = lax.dot_general(eye_l, a_cs, NT, precision=hi, preferred_element_type=F32)
    last = a_cs[Q - 1:Q, :]
    exp_cs = jnp.exp(a_cs)
    decay = jnp.exp(last - a_cs)
    chunk_decay = jnp.exp(last)
    dsk = dsk_ref[...]
    eye_n = (lax.broadcasted_iota(I32, (N, N), 0) == lax.broadcasted_iota(I32, (N, N), 1)).astype(BF16)

    for g in range(G):
        b16 = xbc_ref[:, DI + g * N:DI + (g + 1) * N].astype(BF16)
        c16 = xbc_ref[:, DI + (G + g) * N:DI + (G + g + 1) * N].astype(BF16)
        cb = lax.dot_general(c16, b16, NT, preferred_element_type=F32)
        b16_t = lax.dot_general(eye_n, b16, NT, preferred_element_type=F32).astype(BF16)
        slabs = [(g * R + r0) // per for r0 in range(0, R, per)]
        xs_l, xw16, st_l, ecs_l, cdec_l, xm16, m16_l = [], [], [], [], [], [], []
        for sl in slabs:
            h0 = sl * per
            xs_w = xbc_ref[:, sl * LANES:(sl + 1) * LANES]

            def wide(v, h0=h0):
                out = v[:, h0:h0 + 1]
                for hh in range(1, per):
                    out = jnp.where(lane_q >= hh * P, v[:, h0 + hh:h0 + hh + 1], out)
                return out

            xdt_w = xs_w * wide(dt)
            xs_l.append(xs_w)
            xw16.append((xdt_w * wide(decay)).astype(BF16))
            ecs_l.append(wide(exp_cs))
            cdec_l.append(wide(chunk_decay)[0:1])
            st_l.append(st_ref[sl])
            for hh in range(per):
                h = h0 + hh
                seg = a_cs[:, h:h + 1] - a_cs_t[h:h + 1, :]
                lmat = jnp.where(tril, jnp.exp(jnp.where(tril, seg, 0.0)), 0.0)
                m16_l.append((cb * lmat).astype(BF16))
                own = (lane_q >= hh * P) & (lane_q < (hh + 1) * P)
                xm16.append(jnp.where(own, xdt_w, 0.0).astype(BF16))
        ns = len(slabs)
        y_off = [jnp.dot(c16, st_l[i].astype(BF16), preferred_element_type=F32) for i in range(ns)]
        y_dia = [jnp.dot(m16_l[i], xm16[i], preferred_element_type=F32) for i in range(ns * per)]
        s_new = [jnp.dot(b16_t, xw16[i], preferred_element_type=F32) for i in range(ns)]
        for i, sl in enumerate(slabs):
            y = y_off[i] * ecs_l[i] + xs_l[i] * dsk[:, sl * LANES:(sl + 1) * LANES]
            for hh in range(per):
                y = y + y_dia[i * per + hh]
            y_ref[:, sl * LANES:(sl + 1) * LANES] = y
            st_ref[sl] = st_l[i] * cdec_l[i] + s_new[i]
    fin_ref[0] = st_ref[...]


def ssd_scan(xbc, dt_raw, dt_bias, a_log, d_skip, init_t, first, seq, H, P, N, G):
    T, C = xbc.shape
    nseq = init_t.shape[0]
    DI = H * P
    NS = DI // LANES
    kern = functools.partial(_ssd_kernel, H=H, P=P, N=N, G=G)
    grid_spec = pltpu.PrefetchScalarGridSpec(
        num_scalar_prefetch=2,
        grid=(T // CHUNK,),
        in_specs=[pl.BlockSpec((CHUNK, C), lambda c, f, s: (c, 0)),
                  pl.BlockSpec((CHUNK, LANES), lambda c, f, s: (c, 0)),
                  pl.BlockSpec((1, LANES), lambda c, f, s: (0, 0)),
                  pl.BlockSpec((1, LANES), lambda c, f, s: (0, 0)),
                  pl.BlockSpec((1, DI), lambda c, f, s: (0, 0)),
                  pl.BlockSpec((1, NS, N, LANES), lambda c, f, s: (s[c], 0, 0, 0))],
        out_specs=[pl.BlockSpec((CHUNK, DI), lambda c, f, s: (c, 0)),
                   pl.BlockSpec((1, NS, N, LANES), lambda c, f, s: (s[c], 0, 0, 0))],
        scratch_shapes=[pltpu.VMEM((NS, N, LANES), F32)],
    )
    return pl.pallas_call(
        kern,
        grid_spec=grid_spec,
        out_shape=[jax.ShapeDtypeStruct((T, DI), F32), jax.ShapeDtypeStruct((nseq, NS, N, LANES), F32)],
        compiler_params=_cparams(("arbitrary",)),
        name="ssd_scan",
    )(first, seq, xbc, dt_raw, dt_bias, a_log, d_skip, init_t)


def _mamba_out_kernel(y_ref, z_ref, x_ref, nw_ref, w_ref, g_ref, b_ref, o_ref, *, groups, alpha):
    z = z_ref[...]
    h = y_ref[...] * (z * jax.nn.sigmoid(z))
    gs = h.shape[1] // groups
    parts = []
    for gi in range(groups):
        hg = h[:, gi * gs:(gi + 1) * gs]
        parts.append(hg * lax.rsqrt(jnp.mean(hg * hg, -1, keepdims=True) + RMS_EPS))
    hn = jnp.concatenate(parts, axis=1) * nw_ref[...]
    out = jnp.dot(hn.astype(BF16), w_ref[...], preferred_element_type=F32)
    o_ref[...] = _layer_norm(alpha * x_ref[...] + out, g_ref[...], b_ref[...])


def mamba_out(y, z, x, norm_w, w_out, g, b, groups, alpha, bm=256):
    T, DI = y.shape
    D = x.shape[1]
    bm = _pick(T, bm)
    return pl.pallas_call(
        functools.partial(_mamba_out_kernel, groups=groups, alpha=alpha),
        grid=(T // bm,),
        in_specs=[pl.BlockSpec((bm, DI), lambda i: (i, 0)),
                  pl.BlockSpec((bm, DI), lambda i: (i, 0)),
                  pl.BlockSpec((bm, D), lambda i: (i, 0)),
                  pl.BlockSpec((1, DI), lambda i: (0, 0)),
                  pl.BlockSpec((DI, D), lambda i: (0, 0)),
                  pl.BlockSpec((1, D), lambda i: (0, 0)),
                  pl.BlockSpec((1, D), lambda i: (0, 0))],
        out_specs=pl.BlockSpec((bm, D), lambda i: (i, 0)),
        out_shape=jax.ShapeDtypeStruct((T, D), F32),
        compiler_params=_cparams(("arbitrary",)),
        name="mamba_out",
    )(y, z, x, norm_w, w_out, g, b)


def _peer_topk_kernel(x_ref, wq_ref, sk_ref, idx_ref, gate_ref, *, nk, half, rows_per_expert):
    q = jnp.dot(x_ref[...].astype(BF16), wq_ref[...], preferred_element_type=F32)
    bt = q.shape[0]
    K = PEER_TOPK
    cw = min(bt, LANES)
    neg = jnp.float32(-jnp.inf)
    iota_k = lax.broadcasted_iota(I32, (nk, cw), 0).astype(F32)

    def fiota(rows):
        return lax.broadcasted_iota(I32, (rows, cw), 0).astype(F32)

    nexp = nk * nk
    assert K * K * nexp < 2 ** 24
    chunks = range(bt // cw)
    cands, keys = [], []
    chains = [(c, j) for c in chunks for j in range(2)]
    ss, v_rows, i_rows = {}, {}, {}
    for c, j in chains:
        qc = q[c * cw:(c + 1) * cw, j * half:(j + 1) * half].astype(BF16)
        ss[c, j] = lax.dot_general(sk_ref[j], qc, NT, preferred_element_type=F32)
        v_rows[c, j], i_rows[c, j] = [], []
    for _ in range(K):
        for ch in chains:
            m = jnp.max(ss[ch], axis=0, keepdims=True)
            am = jnp.min(jnp.where(ss[ch] == m, iota_k, float(nk)), axis=0, keepdims=True)
            v_rows[ch].append(m)
            i_rows[ch].append(am)
            ss[ch] = jnp.where(iota_k == am, neg, ss[ch])
    for c in chunks:
        sv = [jnp.concatenate(v_rows[c, j], axis=0) for j in range(2)]
        si = [jnp.concatenate(i_rows[c, j], axis=0) for j in range(2)]
        vals, kys = [], []
        a = 0
        while a < K and K // (a + 1) >= 2:
            n_a = K // (a + 1)
            rows = _round_up(n_a, 8)
            b_pos = fiota(rows)
            vals.append(jnp.where(b_pos < n_a, sv[0][a:a + 1] + sv[1][:rows], neg))
            kys.append((b_pos + float(a * K)) * float(nexp) + (si[0][a:a + 1] * float(nk) + si[1][:rows]))
            a += 1
        if a < K:
            vals.append(sv[0][a:K] + sv[1][0:1])
            kys.append((fiota(K - a) + float(a)) * float(K * nexp) + (si[0][a:K] * float(nk) + si[1][0:1]))
        cands.append(jnp.concatenate(vals, axis=0))
        keys.append(jnp.concatenate(kys, axis=0))
    best = [[] for _ in chunks]
    kmin = [[] for _ in chunks]
    for _ in range(K):
        for c in chunks:
            m = jnp.max(cands[c], axis=0, keepdims=True)
            kk = jnp.min(jnp.where(cands[c] == m, keys[c], float(2 ** 24)), axis=0, keepdims=True)
            best[c].append(m)
            kmin[c].append(kk)
            cands[c] = jnp.where(keys[c] == kk, neg, cands[c])
    for c in chunks:
        bestv = jnp.concatenate(best[c], axis=0)
        pr = jnp.exp(bestv - bestv[0:1])
        gate_ref[:, c * cw:(c + 1) * cw] = pr / jnp.sum(pr, axis=0, keepdims=True)
        ki = jnp.concatenate(kmin[c], axis=0).astype(I32)
        eid = (ki & (nexp - 1)) if nexp & (nexp - 1) == 0 else lax.rem(ki, nexp)
        idx_ref[:, c * cw:(c + 1) * cw] = eid * rows_per_expert


def peer_topk(x, wq, sk, rows_per_expert, bt=256):
    T, D = x.shape
    nk, half = sk.shape[1], sk.shape[2]
    nh = wq.shape[1] // (2 * half)
    K = PEER_TOPK
    bt = _pick(T, bt)
    kern = functools.partial(_peer_topk_kernel, nk=nk, half=half, rows_per_expert=rows_per_expert)
    return pl.pallas_call(
        kern,
        grid=(T // bt, nh),
        in_specs=[pl.BlockSpec((bt, D), lambda i, h: (i, 0)),
                  pl.BlockSpec((D, 2 * half), lambda i, h: (0, h)),
                  pl.BlockSpec((2, nk, half), lambda i, h: (0, 0, 0))],
        out_specs=[pl.BlockSpec((K, bt), lambda i, h: (h, i)),
                   pl.BlockSpec((K, bt), lambda i, h: (h, i))],
        out_shape=[jax.ShapeDtypeStruct((nh * K, T), I32), jax.ShapeDtypeStruct((nh * K, T), F32)],
        compiler_params=_cparams(("arbitrary", "arbitrary")),
        name="peer_topk",
    )(x, wq, sk)


GROUP = 8


def _gather_rows(idx_ref, t0, tab_ref, tile_ref, nsel, p):
    rows = [idx_ref.at[t0 + i] for i in range(GROUP)]
    for k in range(nsel):
        for i in range(GROUP):
            e = pl.multiple_of(rows[i][k], p)
            tile_ref[pl.ds((i * nsel + k) * p, p), :] = tab_ref[pl.ds(e, p), :]


def _unpack_words(tile_ref, s, i, c, nsel, p):
    w = tile_ref[s, pl.ds(i * nsel * p + c, nsel, stride=p), :] if p > 1 else tile_ref[s, pl.ds(i * nsel, nsel), :]
    lo = lax.bitcast_convert_type(w << 16, F32)
    hi = lax.bitcast_convert_type(w & jnp.int32(-65536), F32)
    return lo, hi


def _peer_schedule(idx_ref, nxt_ref, tab_ref, tile_ref, bt, nsel, p, consume, carry):
    @pl.when(pl.program_id(0) == 0)
    def _():
        _gather_rows(idx_ref, 0, tab_ref, tile_ref.at[0], nsel, p)

    def body(j, c):
        for s in range(2):
            t0 = (2 * j + s) * GROUP
            _gather_rows(idx_ref, t0 + GROUP, tab_ref, tile_ref.at[1 - s], nsel, p)
            c = consume(t0, s, c)
        return c

    ntrip = bt // (2 * GROUP)
    carry = lax.fori_loop(0, ntrip - 1, body, carry)
    t0 = (ntrip - 1) * 2 * GROUP
    _gather_rows(idx_ref, t0 + GROUP, tab_ref, tile_ref.at[1], nsel, p)
    carry = consume(t0, 0, carry)
    _gather_rows(nxt_ref, 0, tab_ref, tile_ref.at[0], nsel, p)
    return consume(t0 + GROUP, 1, carry)


def _peer_u_kernel(idx_ref, nxt_ref, x_ref, gate_ref, tab_ref, act_ref, tile_ref, *, nsel, p):
    bt = x_ref.shape[0]
    lane = lax.broadcasted_iota(I32, (nsel, bt), 1)

    def consume(t0, s, acc):
        for i in range(GROUP):
            t = t0 + i
            xrow = x_ref[pl.ds(t, 1), :]
            part = None
            for c in range(p):
                lo, hi = _unpack_words(tile_ref, s, i, c, nsel, p)
                term = (lo * xrow[:, c * LANES:(c + 1) * LANES]
                        + hi * xrow[:, (p + c) * LANES:(p + c + 1) * LANES])
                part = term if part is None else part + term
            acc = jnp.where(lane == t, jnp.sum(part, axis=1, keepdims=True), acc)
        return acc

    hpre = _peer_schedule(idx_ref, nxt_ref, tab_ref, tile_ref, bt, nsel, p, consume,
                          jnp.zeros((nsel, bt), F32))
    act_ref[...] = 0.5 * hpre * (1.0 + lax.erf(hpre * (2.0 ** -0.5))) * gate_ref[...]


def peer_u_phase(idx_t, x, gates, tab, p, bt=256):
    T, nsel = idx_t.shape
    D = x.shape[1]
    bt = _pick(T, bt)
    kern = functools.partial(_peer_u_kernel, nsel=nsel, p=p)
    return pl.pallas_call(
        kern,
        grid=(T // bt,),
        in_specs=[pl.BlockSpec((bt, nsel), lambda i: (i, 0), memory_space=pltpu.SMEM),
                  pl.BlockSpec((bt, nsel), lambda i: (jnp.minimum(i + 1, T // bt - 1), 0), memory_space=pltpu.SMEM),
                  pl.BlockSpec((bt, D), lambda i: (i, 0)),
                  pl.BlockSpec((nsel, bt), lambda i: (0, i)),
                  pl.BlockSpec(tab.shape, lambda i: (0, 0), pipeline_mode=pl.Buffered(1))],
        out_specs=pl.BlockSpec((nsel, bt), lambda i: (0, i)),
        out_shape=jax.ShapeDtypeStruct((nsel, T), F32),
        scratch_shapes=[pltpu.VMEM((2, GROUP * nsel * p, LANES), I32)],
        compiler_params=_cparams(("arbitrary",), VMEM_TABLE_LIMIT),
        name="peer_u",
    )(idx_t, idx_t, x, gates, tab)


def _peer_v_kernel(idx_ref, nxt_ref, act_ref, x_ref, tab_ref, g_ref, b_ref, o_ref, tile_ref, out_ref, *, nsel, p, alpha):
    bt = x_ref.shape[0]
    lane = lax.broadcasted_iota(I32, (nsel, bt), 1)

    def consume(t0, s, carry):
        for i in range(GROUP):
            t = t0 + i
            a = jnp.sum(jnp.where(lane == t, act_ref[...], 0.0), axis=1, keepdims=True)
            los, his = [], []
            for c in range(p):
                lo, hi = _unpack_words(tile_ref, s, i, c, nsel, p)
                los.append(jnp.sum(lo * a, axis=0, keepdims=True))
                his.append(jnp.sum(hi * a, axis=0, keepdims=True))
            out_ref[pl.ds(t, 1), :] = jnp.concatenate(los + his, axis=1)
        return carry

    _peer_schedule(idx_ref, nxt_ref, tab_ref, tile_ref, bt, nsel, p, consume, 0)
    o_ref[...] = _layer_norm(alpha * x_ref[...] + out_ref[...], g_ref[...], b_ref[...])


def peer_v_phase(idx_t, act, x, tab, g, b, p, alpha, bt=256):
    T, nsel = idx_t.shape
    D = x.shape[1]
    bt = _pick(T, bt)
    kern = functools.partial(_peer_v_kernel, nsel=nsel, p=p, alpha=alpha)
    return pl.pallas_call(
        kern,
        grid=(T // bt,),
        in_specs=[pl.BlockSpec((bt, nsel), lambda i: (i, 0), memory_space=pltpu.SMEM),
                  pl.BlockSpec((bt, nsel), lambda i: (jnp.minimum(i + 1, T // bt - 1), 0), memory_space=pltpu.SMEM),
                  pl.BlockSpec((nsel, bt), lambda i: (0, i)),
                  pl.BlockSpec((bt, D), lambda i: (i, 0)),
                  pl.BlockSpec(tab.shape, lambda i: (0, 0), pipeline_mode=pl.Buffered(1)),
                  pl.BlockSpec((1, D), lambda i: (0, 0)),
                  pl.BlockSpec((1, D), lambda i: (0, 0))],
        out_specs=pl.BlockSpec((bt, D), lambda i: (i, 0)),
        out_shape=jax.ShapeDtypeStruct((T, D), F32),
        scratch_shapes=[pltpu.VMEM((2, GROUP * nsel * p, LANES), I32), pltpu.VMEM((bt, D), F32)],
        compiler_params=_cparams(("arbitrary",), VMEM_TABLE_LIMIT),
        name="peer_v",
    )(idx_t, idx_t, act, x, tab, g, b)


def pack_table(tab):
    E, D = tab.shape
    bits = lax.bitcast_convert_type(tab.astype(BF16), jnp.uint16).astype(jnp.uint32)
    words = bits[:, :D // 2] | (bits[:, D // 2:] << 16)
    return lax.bitcast_convert_type(words, I32).reshape(E * (D // 256), LANES)


def _kv_kernel(x_ref, wkv_ref, wf_ref, wft_ref, bf_ref, bft_ref, k_ref, v_ref, k16_ref, v16_ref, lf_ref, lft_ref, *, aw):
    xb = x_ref[...].astype(BF16)
    kv = jnp.dot(xb, wkv_ref[...], preferred_element_type=F32)
    k = kv[:, :aw]
    v = kv[:, aw:]
    k_ref[...] = k
    v_ref[...] = v
    k16_ref[...] = k.astype(BF16)
    v16_ref[...] = v.astype(BF16)
    f = jnp.dot(xb, wf_ref[...], preferred_element_type=F32) + bf_ref[...]
    lf_ref[...] = jax.nn.log_sigmoid(f)
    ft = lax.dot_general(wft_ref[...], xb, NT, preferred_element_type=F32) + bft_ref[...]
    lft_ref[...] = jax.nn.log_sigmoid(ft)


def shared_kv(x, wkv, wf, wft, bf, bft, bm=512):
    T, D = x.shape
    aw = wkv.shape[1] // 2
    H = wf.shape[1]
    bm = _pick(T, bm)
    outs = [jax.ShapeDtypeStruct((T, aw), F32), jax.ShapeDtypeStruct((T, aw), F32),
            jax.ShapeDtypeStruct((T, aw), BF16), jax.ShapeDtypeStruct((T, aw), BF16),
            jax.ShapeDtypeStruct((T, H), F32), jax.ShapeDtypeStruct((H, T), F32)]
    row = lambda i: (i, 0)
    fixed = lambda i: (0, 0)
    return pl.pallas_call(
        functools.partial(_kv_kernel, aw=aw),
        grid=(T // bm,),
        in_specs=[pl.BlockSpec((bm, D), row), pl.BlockSpec((D, 2 * aw), fixed), pl.BlockSpec((D, H), fixed),
                  pl.BlockSpec((H, D), fixed), pl.BlockSpec((1, H), fixed), pl.BlockSpec((H, 1), fixed)],
        out_specs=[pl.BlockSpec((bm, aw), row), pl.BlockSpec((bm, aw), row), pl.BlockSpec((bm, aw), row),
                   pl.BlockSpec((bm, aw), row), pl.BlockSpec((bm, H), row), pl.BlockSpec((H, bm), lambda i: (0, i))],
        out_shape=outs,
        compiler_params=_cparams(("arbitrary",)),
        name="shared_kv",
    )(x, wkv, wf, wft, bf, bft)


def _cumsum_kernel(lf_ref, lft_ref, ck_ref, ckt_ref):
    L = lf_ref.shape[1]
    B = LANES
    hi = lax.Precision.HIGHEST
    r = lax.broadcasted_iota(I32, (B, B), 0)
    c = lax.broadcasted_iota(I32, (B, B), 1)
    lower = (r >= c).astype(F32)
    upper = (r <= c).astype(F32)
    H = lf_ref.shape[2]
    carry = jnp.zeros((1, H), F32)
    carry_t = jnp.zeros((H, 1), F32)
    for j in range(L // B):
        blk = jnp.dot(lower, lf_ref[0, j * B:(j + 1) * B, :], precision=hi, preferred_element_type=F32) + carry
        ck_ref[0, j * B:(j + 1) * B, :] = blk
        carry = blk[B - 1:B, :]
        blk_t = jnp.dot(lft_ref[0, :, j * B:(j + 1) * B], upper, precision=hi, preferred_element_type=F32) + carry_t
        ckt_ref[0, :, j * B:(j + 1) * B] = blk_t
        carry_t = blk_t[:, B - 1:B]


def forget_cumsum(lf, lft):
    nseq, L, H = lf.shape
    return pl.pallas_call(
        _cumsum_kernel,
        grid=(nseq,),
        in_specs=[pl.BlockSpec((1, L, H), lambda s: (s, 0, 0)), pl.BlockSpec((1, H, L), lambda s: (s, 0, 0))],
        out_specs=[pl.BlockSpec((1, L, H), lambda s: (s, 0, 0)), pl.BlockSpec((1, H, L), lambda s: (s, 0, 0))],
        out_shape=[jax.ShapeDtypeStruct((nseq, L, H), F32), jax.ShapeDtypeStruct((nseq, H, L), F32)],
        compiler_params=_cparams(("arbitrary",)),
        name="forget_cumsum",
    )(lf, lft)


FOX_SUB_ROWS = 128


def _fox_kernel(q_ref, g_ref, k_ref, v_ref, cq_ref, ckt_ref, o_ref, *, bq, bk, dh, off, scale):
    hp = pl.program_id(1)
    qi = pl.program_id(2)
    rs = FOX_SUB_ROWS if bq % FOX_SUB_ROWS == 0 else bq
    nsub = bq // rs
    lane = lax.broadcasted_iota(I32, (rs, 2 * dh), 1)
    rowi = lax.broadcasted_iota(I32, (rs, bk), 0)
    kiota = lax.broadcasted_iota(I32, (rs, bk), 1)
    nkb = (off + (qi + 1) * bq + bk - 1) // bk
    head_lane = lax.broadcasted_iota(I32, (rs, cq_ref.shape[2]), 1)
    qh, cq, qpos = [], [], []
    for sb in range(nsub):
        q2 = q_ref[sb * rs:(sb + 1) * rs, :] * scale
        cq_all = cq_ref[0, sb * rs:(sb + 1) * rs, :]
        qpos.append(off + qi * bq + sb * rs + rowi)
        for hh in range(2):
            sel = (lane >= dh) if hh else (lane < dh)
            qh.append(jnp.where(sel, q2, 0.0).astype(BF16))
            cq.append(jnp.sum(jnp.where(head_lane == 2 * hp + hh, cq_all, 0.0), axis=1, keepdims=True))

    def body(kb, carry):
        ks = pl.multiple_of(kb * bk, bk)
        kblk = k_ref[0, pl.ds(ks, bk), :]
        vblk = v_ref[0, pl.ds(ks, bk), :]
        cks = [ckt_ref[0, 2 * hp + hh, :, pl.ds(ks, bk)] for hh in range(2)]
        nch = 2 * nsub
        scores = [lax.dot_general(qh[ch], kblk, NT, preferred_element_type=F32) for ch in range(nch)]
        stats, probs = [], []
        for ch in range(nch):
            m, l, _ = carry[ch]
            causal = kiota + ks <= qpos[ch // 2]
            s = jnp.where(causal, scores[ch] + cq[ch] - cks[ch % 2], -jnp.inf)
            m_new = jnp.maximum(m, jnp.max(s, axis=1, keepdims=True))
            alpha = jnp.exp(m - m_new)
            pexp = jnp.exp(s - m_new)
            stats.append((m_new, alpha * l + jnp.sum(pexp, axis=1, keepdims=True), alpha))
            probs.append(pexp.astype(BF16))
        new = []
        for ch in range(nch):
            m_new, l_new, alpha = stats[ch]
            acc = alpha * carry[ch][2] + jnp.dot(probs[ch], vblk, preferred_element_type=F32)
            new.append((m_new, l_new, acc))
        return tuple(new)

    init = (jnp.full((rs, 1), -jnp.inf, F32), jnp.zeros((rs, 1), F32), jnp.zeros((rs, 2 * dh), F32))
    fin = lax.fori_loop(0, nkb, body, (init,) * (2 * nsub))
    for sb in range(nsub):
        (_, l0, a0), (_, l1, a1) = fin[2 * sb], fin[2 * sb + 1]
        o = jnp.where(lane < dh, a0 / l0, a1 / l1)
        o_ref[sb * rs:(sb + 1) * rs, :] = o * jax.nn.sigmoid(g_ref[sb * rs:(sb + 1) * rs, :])


def fox_attention(qg, k16, v16, ck, ckt, tok0, nseq, L, Lk, dh, bq, bk):
    aw = k16.shape[2]
    Lkp = k16.shape[1]
    H = ck.shape[2]
    off = Lk - L
    nq = L // bq
    tb0 = tok0 // bq
    npair = aw // (2 * dh)
    kern = functools.partial(_fox_kernel, bq=bq, bk=bk, dh=dh, off=off, scale=dh ** -0.5)
    return pl.pallas_call(
        kern,
        grid=(nseq, npair, nq),
        in_specs=[pl.BlockSpec((bq, 2 * dh), lambda s, h, i: (tb0 + s * nq + i, h)),
                  pl.BlockSpec((bq, 2 * dh), lambda s, h, i: (tb0 + s * nq + i, npair + h)),
                  pl.BlockSpec((1, Lkp, 2 * dh), lambda s, h, i: (s, 0, h)),
                  pl.BlockSpec((1, Lkp, 2 * dh), lambda s, h, i: (s, 0, h)),
                  pl.BlockSpec((1, bq, H), lambda s, h, i: (s, off // bq + i, 0)),
                  pl.BlockSpec((1, H, 1, Lkp), lambda s, h, i: (s, 0, 0, 0))],
        out_specs=pl.BlockSpec((bq, 2 * dh), lambda s, h, i: (s * nq + i, h)),
        out_shape=jax.ShapeDtypeStruct((nseq * L, aw), F32),
        compiler_params=_cparams(("arbitrary", "arbitrary", "arbitrary")),
        name="fox_attention",
    )(qg, qg, k16, v16, ck, ckt.reshape(nseq, H, 1, Lkp))


def _pad_lanes(v, n=LANES):
    v = v.reshape(1, -1).astype(F32)
    return jnp.pad(v, ((0, 0), (0, n - v.shape[1])))


def _round_up(n, m):
    return (n + m - 1) // m * m


def kernel(x_prompt, x_sample, state_ssm, state_conv, cache_k, cache_v, cache_logf, a_w_in, a_conv_w, a_conv_b, a_dt_bias, a_A_log, a_D, a_norm_w, a_w_out, kv_w, kv_b_f, b_w_qg, b_w_o, peer_w_q, peer_subkeys, peer_u, peer_v, ln_g, ln_b):
    Bp, S, D = x_prompt.shape
    Bs, Ss, _ = x_sample.shape
    depth = ln_g.shape[0]
    n_a = a_w_in.shape[0]
    H = a_dt_bias.shape[1]
    DI = a_w_out.shape[1]
    P = DI // H
    N = state_ssm.shape[-1]
    C = a_conv_w.shape[2]
    G = (C - DI) // (2 * N)
    past = cache_k.shape[1]
    AH, dh = cache_k.shape[2], cache_k.shape[3]
    AW = AH * dh
    alpha = (2.0 * depth) ** 0.25
    Tp, Ts = Bp * S, Bs * Ss
    T = Tp + Ts
    nseq = Bp + Bs
    assert S % CHUNK == 0 and Ss % CHUNK == 0 and H <= LANES and D % 256 == 0 and LANES % P == 0
    per = LANES // P
    assert (H // G) % per == 0

    seq_np = np.concatenate([np.repeat(np.arange(Bp), S // CHUNK), Bp + np.repeat(np.arange(Bs), Ss // CHUNK)])
    first_np = np.concatenate([[1], (seq_np[1:] != seq_np[:-1]).astype(np.int64)])
    seq = jnp.asarray(seq_np, I32)
    first = jnp.asarray(first_np, I32)

    x = jnp.concatenate([x_prompt.reshape(Tp, D), x_sample.reshape(Ts, D)], axis=0)
    rows_per_expert = D // 256
    new_conv, new_ssm = [], []
    k_f32 = v_f32 = lf = None

    for i in range(depth):
        g1, b1 = ln_g[i, 0].reshape(1, D), ln_b[i, 0].reshape(1, D)
        g2, b2 = ln_g[i, 1].reshape(1, D), ln_b[i, 1].reshape(1, D)
        if i < n_a:
            w_in = a_w_in[i].astype(BF16)
            z = matmul(x, w_in[:, :DI])
            xbc_raw = matmul(x, w_in[:, DI:DI + C])
            w_dt = jnp.pad(w_in[:, DI + C:], ((0, 0), (0, LANES - H)))
            dt_raw = matmul(x, w_dt)
            conv0 = jnp.concatenate([jnp.zeros((Bp,) + state_conv.shape[2:], F32), state_conv[i]], axis=0)
            xbc, conv_new = conv_silu(xbc_raw, conv0, a_conv_w[i], a_conv_b[i].reshape(1, C), first, seq)
            ssm0 = jnp.concatenate([jnp.zeros((Bp, H, P, N), F32), state_ssm[i]], axis=0)
            ssm0_t = ssm0.reshape(nseq, H // per, per, P, N).transpose(0, 1, 4, 2, 3).reshape(nseq, H // per, N, LANES)
            y, ssm_new = ssd_scan(xbc, dt_raw, _pad_lanes(a_dt_bias[i]), _pad_lanes(a_A_log[i]),
                                  jnp.repeat(a_D[i].astype(F32), P).reshape(1, DI), ssm0_t, first, seq, H, P, N, G)
            new_conv.append(conv_new)
            new_ssm.append(ssm_new.reshape(nseq, H // per, N, per, P).transpose(0, 1, 3, 4, 2).reshape(nseq, H, P, N))
            x = mamba_out(y, z, x, a_norm_w[i].reshape(1, DI), a_w_out[i].astype(BF16), g1, b1, G, alpha)
        else:
            j = i - n_a
            qg = matmul(x, b_w_qg[j].astype(BF16))
            o_p = fox_attention(qg, kp16, vp16, ck_p, ckt_p, 0, Bp, S, S, dh, bq_p, bk_p)
            o_s = fox_attention(qg, ks16, vs16, ck_s, ckt_s, Tp, Bs, Ss, past + Ss, dh, bq_s, bk_s)
            x = matmul_res_ln(jnp.concatenate([o_p, o_s], axis=0), b_w_o[j].astype(BF16), x, g1, b1, alpha)

        idx, gates = peer_topk(x, peer_w_q[i].astype(BF16), peer_subkeys[i].astype(BF16), rows_per_expert)
        idx_t = idx.T
        act = peer_u_phase(idx_t, x, gates, pack_table(peer_u[i]), rows_per_expert)
        x = peer_v_phase(idx_t, act, x, pack_table(peer_v[i]), g2, b2, rows_per_expert, alpha)

        if i == n_a - 1:
            wkv = kv_w[:, :2 * AW].astype(BF16)
            wf = kv_w[:, 2 * AW:].astype(BF16)
            k_f32, v_f32, k16, v16, lf, lft = shared_kv(x, wkv, wf, wf.T, kv_b_f.reshape(1, AH), kv_b_f.reshape(AH, 1))
            bq_p = _pick(S, 2 * FOX_SUB_ROWS)
            bk_p = _pick(S, 512)
            kp16 = k16[:Tp].reshape(Bp, S, AW)
            vp16 = v16[:Tp].reshape(Bp, S, AW)
            Sp = _round_up(S, LANES)
            lf_p = jnp.pad(lf[:Tp].reshape(Bp, S, AH), ((0, 0), (0, Sp - S), (0, 0)))
            lft_p = jnp.pad(jnp.swapaxes(lft[:, :Tp].reshape(AH, Bp, S), 0, 1), ((0, 0), (0, 0), (0, Sp - S)))
            ck_p, ckt_p = forget_cumsum(lf_p, lft_p)
            Lk = past + Ss
            bq_s = _pick(Ss, 128)
            Lkp = _round_up(Lk, LANES)
            bk_s = max(b for b in range(LANES, 4 * LANES + 1, LANES) if Lkp % b == 0)
            padk = ((0, 0), (0, Lkp - Lk), (0, 0))
            ks16 = jnp.pad(jnp.concatenate([cache_k.reshape(Bs, past, AW).astype(BF16), k16[Tp:].reshape(Bs, Ss, AW)], axis=1), padk)
            vs16 = jnp.pad(jnp.concatenate([cache_v.reshape(Bs, past, AW).astype(BF16), v16[Tp:].reshape(Bs, Ss, AW)], axis=1), padk)
            lf_s = jnp.pad(jnp.concatenate([cache_logf.astype(F32), lf[Tp:].reshape(Bs, Ss, AH)], axis=1), padk)
            lft_new = jnp.swapaxes(lft[:, Tp:].reshape(AH, Bs, Ss), 0, 1)
            lft_s = jnp.pad(jnp.concatenate([jnp.swapaxes(cache_logf.astype(F32), 1, 2), lft_new], axis=2),
                            ((0, 0), (0, 0), (0, Lkp - Lk)))
            ck_s, ckt_s = forget_cumsum(lf_s, lft_s)

    conv_all = jnp.stack(new_conv)
    ssm_all = jnp.stack(new_ssm)
    return (x[:Tp].reshape(Bp, S, D), x[Tp:].reshape(Bs, Ss, D),
            ssm_all[:, :Bp], conv_all[:, :Bp],
            k_f32[:Tp].reshape(Bp, S, AH, dh), v_f32[:Tp].reshape(Bp, S, AH, dh), lf[:Tp].reshape(Bp, S, AH),
            ssm_all[:, Bp:], conv_all[:, Bp:],
            k_f32[Tp:].reshape(Bs, Ss, AH, dh), v_f32[Tp:].reshape(Bs, Ss, AH, dh), lf[Tp:].reshape(Bs, Ss, AH))
```

```python
import functools
import math

import numpy as np
import jax
import jax.numpy as jnp
from jax import lax
from jax.experimental import pallas as pl
from jax.experimental.pallas import tpu as pltpu

F32 = jnp.float32
BF16 = jnp.bfloat16
I32 = jnp.int32

CHUNK = 64
PEER_TOPK = 16
RMS_EPS = 1e-5
LN_EPS = 1e-5
LANES = 128
VMEM_TABLE_LIMIT = 56 * 1024 * 1024
NT = (((1,), (1,)), ((), ()))


def _pick(n, pref):
    if n <= pref:
        return n
    for b in range(pref, 7, -1):
        if n % b == 0 and b % 8 == 0:
            return b
    return n


def _layer_norm(v, g, b):
    mu = jnp.mean(v, -1, keepdims=True)
    d = v - mu
    var = jnp.mean(d * d, -1, keepdims=True)
    return d * lax.rsqrt(var + LN_EPS) * g + b


def _cparams(sem, vmem=None):
    kw = dict(dimension_semantics=sem)
    if vmem is not None:
        kw["vmem_limit_bytes"] = vmem
    return pltpu.CompilerParams(**kw)


def _mm_kernel(x_ref, w_ref, o_ref):
    o_ref[...] = jnp.dot(x_ref[...].astype(BF16), w_ref[...], preferred_element_type=F32)


def matmul(x, w, bm=512, bn=1024):
    T, K = x.shape
    N = w.shape[1]
    bm = _pick(T, bm)
    bn = bn if N % bn == 0 else N
    return pl.pallas_call(
        _mm_kernel,
        grid=(N // bn, T // bm),
        in_specs=[pl.BlockSpec((bm, K), lambda j, i: (i, 0)),
                  pl.BlockSpec((K, bn), lambda j, i: (0, j))],
        out_specs=pl.BlockSpec((bm, bn), lambda j, i: (i, j)),
        out_shape=jax.ShapeDtypeStruct((T, N), F32),
        compiler_params=_cparams(("arbitrary", "arbitrary")),
        name="matmul",
    )(x, w)


def _mm_res_ln_kernel(a_ref, w_ref, x_ref, g_ref, b_ref, o_ref, *, alpha):
    y = jnp.dot(a_ref[...].astype(BF16), w_ref[...], preferred_element_type=F32)
    o_ref[...] = _layer_norm(alpha * x_ref[...] + y, g_ref[...], b_ref[...])


def matmul_res_ln(a, w, x, g, b, alpha, bm=512):
    T, K = a.shape
    D = w.shape[1]
    bm = _pick(T, bm)
    return pl.pallas_call(
        functools.partial(_mm_res_ln_kernel, alpha=alpha),
        grid=(T // bm,),
        in_specs=[pl.BlockSpec((bm, K), lambda i: (i, 0)),
                  pl.BlockSpec((K, D), lambda i: (0, 0)),
                  pl.BlockSpec((bm, D), lambda i: (i, 0)),
                  pl.BlockSpec((1, D), lambda i: (0, 0)),
                  pl.BlockSpec((1, D), lambda i: (0, 0))],
        out_specs=pl.BlockSpec((bm, D), lambda i: (i, 0)),
        out_shape=jax.ShapeDtypeStruct((T, D), F32),
        compiler_params=_cparams(("arbitrary",)),
        name="matmul_res_ln",
    )(a, w, x, g, b)


def _ssd_kernel(first_ref, seq_ref, u_ref, cinit_ref, cw_ref, cb_ref, dt_ref, bias_ref, alog_ref, dsk_ref, init_ref,
                y_ref, fin_ref, cst_ref, st_ref, ext_ref, xbc_ref, *, H, P, N, G, dconv):
    c = pl.program_id(0)
    nprev = dconv - 1

    @pl.when(first_ref[c] == 1)
    def _():
        st_ref[...] = init_ref[0]
        ext_ref[8 - nprev:8, :] = cinit_ref[0]

    u = u_ref[...]
    ext_ref[8:8 + CHUNK, :] = u
    cw = cw_ref[...]
    acc = cb_ref[...] + u * cw[dconv - 1:dconv]
    for k in range(nprev):
        acc = acc + ext_ref[8 - nprev + k:8 - nprev + k + CHUNK, :] * cw[k:k + 1]
    xbc_ref[...] = acc * jax.nn.sigmoid(acc)
    tail = ext_ref[8 + CHUNK - nprev:8 + CHUNK, :]
    cst_ref[0] = tail
    ext_ref[8 - nprev:8, :] = tail

    Q = CHUNK
    DI = H * P
    R = H // G
    per = LANES // P
    hi = lax.Precision.HIGHEST
    dt = jax.nn.softplus(dt_ref[...] + bias_ref[...])
    a = dt * (-jnp.exp(alog_ref[...]))
    row = lax.broadcasted_iota(I32, (Q, Q), 0)
    col = lax.broadcasted_iota(I32, (Q, Q), 1)
    tril = row >= col
    lane_q = lax.broadcasted_iota(I32, (Q, LANES), 1)
    a_cs = jnp.dot(tril.astype(F32), a, precision=hi, preferred_element_type=F32)
    eye_l = (lax.broadcasted_iota(I32, (LANES, LANES), 0) == lax.broadcasted_iota(I32, (LANES, LANES), 1)).astype(F32)
    a_cs_t = lax.dot_general(eye_l, a_cs, NT, precision=hi, preferred_element_type=F32)
    last = a_cs[Q - 1:Q, :]
    exp_cs = jnp.exp(a_cs)
    decay = jnp.exp(last - a_cs)
    chunk_decay = jnp.exp(last)
    dsk = dsk_ref[...]
    eye_n = (lax.broadcasted_iota(I32, (N, N), 0) == lax.broadcasted_iota(I32, (N, N), 1)).astype(BF16)

    for g in range(G):
        b16 = xbc_ref[:, DI + g * N:DI + (g + 1) * N].astype(BF16)
        c16 = xbc_ref[:, DI + (G + g) * N:DI + (G + g + 1) * N].astype(BF16)
        cb = lax.dot_general(c16, b16, NT, preferred_element_type=F32)
        b16_t = lax.dot_general(eye_n, b16, NT, preferred_element_type=F32).astype(BF16)
        slabs = [(g * R + r0) // per for r0 in range(0, R, per)]
        xs_l, xw16, st_l, ecs_l, cdec_l, xm16, m16_l = [], [], [], [], [], [], []
        for sl in slabs:
            h0 = sl * per
            xs_w = xbc_ref[:, sl * LANES:(sl + 1) * LANES]

            def wide(v, h0=h0):
                out = v[:, h0:h0 + 1]
                for hh in range(1, per):
                    out = jnp.where(lane_q >= hh * P, v[:, h0 + hh:h0 + hh + 1], out)
                return out

            xdt_w = xs_w * wide(dt)
            xs_l.append(xs_w)
            xw16.append((xdt_w * wide(decay)).astype(BF16))
            ecs_l.append(wide(exp_cs))
            cdec_l.append(wide(chunk_decay)[0:1])
            st_l.append(st_ref[sl])
            for hh in range(per):
                h = h0 + hh
                seg = a_cs[:, h:h + 1] - a_cs_t[h:h + 1, :]
                lmat = jnp.where(tril, jnp.exp(jnp.where(tril, seg, 0.0)), 0.0)
                m16_l.append((cb * lmat).astype(BF16))
                own = (lane_q >= hh * P) & (lane_q < (hh + 1) * P)
                xm16.append(jnp.where(own, xdt_w, 0.0).astype(BF16))
        ns = len(slabs)
        y_off = [jnp.dot(c16, st_l[i].astype(BF16), preferred_element_type=F32) for i in range(ns)]
        y_dia = [jnp.dot(m16_l[i], xm16[i], preferred_element_type=F32) for i in range(ns * per)]
        s_new = [jnp.dot(b16_t, xw16[i], preferred_element_type=F32) for i in range(ns)]
        for i, sl in enumerate(slabs):
            y = y_off[i] * ecs_l[i] + xs_l[i] * dsk[:, sl * LANES:(sl + 1) * LANES]
            for hh in range(per):
                y = y + y_dia[i * per + hh]
            y_ref[:, sl * LANES:(sl + 1) * LANES] = y
            st_ref[sl] = st_l[i] * cdec_l[i] + s_new[i]
    fin_ref[0] = st_ref[...]


def ssd_scan(u, conv_init, conv_w, conv_b, dt_raw, dt_bias, a_log, d_skip, init_t, first, seq, H, P, N, G):
    T, C = u.shape
    nseq, nprev, _ = conv_init.shape
    dconv = nprev + 1
    DI = H * P
    NS = DI // LANES
    kern = functools.partial(_ssd_kernel, H=H, P=P, N=N, G=G, dconv=dconv)
    grid_spec = pltpu.PrefetchScalarGridSpec(
        num_scalar_prefetch=2,
        grid=(T // CHUNK,),
        in_specs=[pl.BlockSpec((CHUNK, C), lambda c, f, s: (c, 0)),
                  pl.BlockSpec((1, nprev, C), lambda c, f, s: (s[c], 0, 0)),
                  pl.BlockSpec((dconv, C), lambda c, f, s: (0, 0)),
                  pl.BlockSpec((1, C), lambda c, f, s: (0, 0)),
                  pl.BlockSpec((CHUNK, LANES), lambda c, f, s: (c, 0)),
                  pl.BlockSpec((1, LANES), lambda c, f, s: (0, 0)),
                  pl.BlockSpec((1, LANES), lambda c, f, s: (0, 0)),
                  pl.BlockSpec((1, DI), lambda c, f, s: (0, 0)),
                  pl.BlockSpec((1, NS, N, LANES), lambda c, f, s: (s[c], 0, 0, 0))],
        out_specs=[pl.BlockSpec((CHUNK, DI), lambda c, f, s: (c, 0)),
                   pl.BlockSpec((1, NS, N, LANES), lambda c, f, s: (s[c], 0, 0, 0)),
                   pl.BlockSpec((1, nprev, C), lambda c, f, s: (s[c], 0, 0))],
        scratch_shapes=[pltpu.VMEM((NS, N, LANES), F32), pltpu.VMEM((8 + CHUNK, C), F32), pltpu.VMEM((CHUNK, C), F32)],
    )
    return pl.pallas_call(
        kern,
        grid_spec=grid_spec,
        out_shape=[jax.ShapeDtypeStruct((T, DI), F32), jax.ShapeDtypeStruct((nseq, NS, N, LANES), F32),
                   jax.ShapeDtypeStruct((nseq, nprev, C), F32)],
        compiler_params=_cparams(("arbitrary",)),
        name="ssd_scan",
    )(first, seq, u, conv_init, conv_w, conv_b, dt_raw, dt_bias, a_log, d_skip, init_t)


def _mamba_out_kernel(y_ref, z_ref, x_ref, nw_ref, w_ref, g_ref, b_ref, o_ref, *, groups, alpha):
    z = z_ref[...]
    h = y_ref[...] * (z * jax.nn.sigmoid(z))
    gs = h.shape[1] // groups
    parts = []
    for gi in range(groups):
        hg = h[:, gi * gs:(gi + 1) * gs]
        parts.append(hg * lax.rsqrt(jnp.mean(hg * hg, -1, keepdims=True) + RMS_EPS))
    hn = jnp.concatenate(parts, axis=1) * nw_ref[...]
    out = jnp.dot(hn.astype(BF16), w_ref[...], preferred_element_type=F32)
    o_ref[...] = _layer_norm(alpha * x_ref[...] + out, g_ref[...], b_ref[...])


def mamba_out(y, z, x, norm_w, w_out, g, b, groups, alpha, bm=256):
    T, DI = y.shape
    D = x.shape[1]
    bm = _pick(T, bm)
    return pl.pallas_call(
        functools.partial(_mamba_out_kernel, groups=groups, alpha=alpha),
        grid=(T // bm,),
        in_specs=[pl.BlockSpec((bm, DI), lambda i: (i, 0)),
                  pl.BlockSpec((bm, DI), lambda i: (i, 0)),
                  pl.BlockSpec((bm, D), lambda i: (i, 0)),
                  pl.BlockSpec((1, DI), lambda i: (0, 0)),
                  pl.BlockSpec((DI, D), lambda i: (0, 0)),
                  pl.BlockSpec((1, D), lambda i: (0, 0)),
                  pl.BlockSpec((1, D), lambda i: (0, 0))],
        out_specs=pl.BlockSpec((bm, D), lambda i: (i, 0)),
        out_shape=jax.ShapeDtypeStruct((T, D), F32),
        compiler_params=_cparams(("arbitrary",)),
        name="mamba_out",
    )(y, z, x, norm_w, w_out, g, b)


def _peer_topk_kernel(x_ref, wq_ref, sk_ref, idx_ref, gate_ref, *, nk, half, rows_per_expert):
    q = jnp.dot(x_ref[...].astype(BF16), wq_ref[...], preferred_element_type=F32)
    bt = q.shape[0]
    K = PEER_TOPK
    cw = min(bt, LANES)
    neg = jnp.float32(-jnp.inf)
    iota_k = lax.broadcasted_iota(I32, (nk, cw), 0).astype(F32)

    def fiota(rows):
        return lax.broadcasted_iota(I32, (rows, cw), 0).astype(F32)

    nexp = nk * nk
    assert K * K * nexp < 2 ** 24
    chunks = range(bt // cw)
    cands, keys = [], []
    chains = [(c, j) for c in chunks for j in range(2)]
    ss, v_rows, i_rows = {}, {}, {}
    for c, j in chains:
        qc = q[c * cw:(c + 1) * cw, j * half:(j + 1) * half].astype(BF16)
        ss[c, j] = lax.dot_general(sk_ref[j], qc, NT, preferred_element_type=F32)
        v_rows[c, j], i_rows[c, j] = [], []
    for _ in range(K):
        for ch in chains:
            m = jnp.max(ss[ch], axis=0, keepdims=True)
            am = jnp.min(jnp.where(ss[ch] == m, iota_k, float(nk)), axis=0, keepdims=True)
            v_rows[ch].append(m)
            i_rows[ch].append(am)
            ss[ch] = jnp.where(iota_k == am, neg, ss[ch])
    for c in chunks:
        sv = [jnp.concatenate(v_rows[c, j], axis=0) for j in range(2)]
        si = [jnp.concatenate(i_rows[c, j], axis=0) for j in range(2)]
        vals, kys = [], []
        a = 0
        while a < K and K // (a + 1) >= 2:
            n_a = K // (a + 1)
            rows = _round_up(n_a, 8)
            b_pos = fiota(rows)
            vals.append(jnp.where(b_pos < n_a, sv[0][a:a + 1] + sv[1][:rows], neg))
            kys.append((b_pos + float(a * K)) * float(nexp) + (si[0][a:a + 1] * float(nk) + si[1][:rows]))
            a += 1
        if a < K:
            vals.append(sv[0][a:K] + sv[1][0:1])
            kys.append((fiota(K - a) + float(a)) * float(K * nexp) + (si[0][a:K] * float(nk) + si[1][0:1]))
        cands.append(jnp.concatenate(vals, axis=0))
        keys.append(jnp.concatenate(kys, axis=0))
    best = [[] for _ in chunks]
    kmin = [[] for _ in chunks]
    for _ in range(K):
        for c in chunks:
            m = jnp.max(cands[c], axis=0, keepdims=True)
            kk = jnp.min(jnp.where(cands[c] == m, keys[c], float(2 ** 24)), axis=0, keepdims=True)
            best[c].append(m)
            kmin[c].append(kk)
            cands[c] = jnp.where(keys[c] == kk, neg, cands[c])
    for c in chunks:
        bestv = jnp.concatenate(best[c], axis=0)
        pr = jnp.exp(bestv - bestv[0:1])
        gate_ref[:, c * cw:(c + 1) * cw] = pr / jnp.sum(pr, axis=0, keepdims=True)
        ki = jnp.concatenate(kmin[c], axis=0).astype(I32)
        eid = (ki & (nexp - 1)) if nexp & (nexp - 1) == 0 else lax.rem(ki, nexp)
        idx_ref[:, c * cw:(c + 1) * cw] = eid * rows_per_expert


def peer_topk(x, wq, sk, rows_per_expert, bt=256):
    T, D = x.shape
    nk, half = sk.shape[1], sk.shape[2]
    nh = wq.shape[1] // (2 * half)
    K = PEER_TOPK
    bt = _pick(T, bt)
    kern = functools.partial(_peer_topk_kernel, nk=nk, half=half, rows_per_expert=rows_per_expert)
    return pl.pallas_call(
        kern,
        grid=(T // bt, nh),
        in_specs=[pl.BlockSpec((bt, D), lambda i, h: (i, 0)),
                  pl.BlockSpec((D, 2 * half), lambda i, h: (0, h)),
                  pl.BlockSpec((2, nk, half), lambda i, h: (0, 0, 0))],
        out_specs=[pl.BlockSpec((K, bt), lambda i, h: (h, i)),
                   pl.BlockSpec((K, bt), lambda i, h: (h, i))],
        out_shape=[jax.ShapeDtypeStruct((nh * K, T), I32), jax.ShapeDtypeStruct((nh * K, T), F32)],
        compiler_params=_cparams(("arbitrary", "arbitrary")),
        name="peer_topk",
    )(x, wq, sk)


GROUP = 8


def _gather_rows(idx_ref, t0, tab_ref, tile_ref, nsel, p):
    rows = [idx_ref.at[t0 + i] for i in range(GROUP)]
    for k in range(nsel):
        for i in range(GROUP):
            e = pl.multiple_of(rows[i][k], p)
            tile_ref[pl.ds((i * nsel + k) * p, p), :] = tab_ref[pl.ds(e, p), :]


def _unpack_words(tile_ref, s, i, c, nsel, p):
    w = tile_ref[s, pl.ds(i * nsel * p + c, nsel, stride=p), :] if p > 1 else tile_ref[s, pl.ds(i * nsel, nsel), :]
    lo = lax.bitcast_convert_type(w << 16, F32)
    hi = lax.bitcast_convert_type(w & jnp.int32(-65536), F32)
    return lo, hi


def _peer_schedule(idx_ref, nxt_ref, tab_ref, tile_ref, bt, nsel, p, consume, carry):
    @pl.when(pl.program_id(0) == 0)
    def _():
        _gather_rows(idx_ref, 0, tab_ref, tile_ref.at[0], nsel, p)

    def body(j, c):
        for s in range(2):
            t0 = (2 * j + s) * GROUP
            _gather_rows(idx_ref, t0 + GROUP, tab_ref, tile_ref.at[1 - s], nsel, p)
            c = consume(t0, s, c)
        return c

    ntrip = bt // (2 * GROUP)
    carry = lax.fori_loop(0, ntrip - 1, body, carry)
    t0 = (ntrip - 1) * 2 * GROUP
    _gather_rows(idx_ref, t0 + GROUP, tab_ref, tile_ref.at[1], nsel, p)
    carry = consume(t0, 0, carry)
    _gather_rows(nxt_ref, 0, tab_ref, tile_ref.at[0], nsel, p)
    return consume(t0 + GROUP, 1, carry)


def _peer_u_kernel(idx_ref, nxt_ref, x_ref, gate_ref, tab_ref, act_ref, tile_ref, *, nsel, p):
    bt = x_ref.shape[0]
    lane = lax.broadcasted_iota(I32, (nsel, bt), 1)

    def consume(t0, s, acc):
        for i in range(GROUP):
            t = t0 + i
            xrow = x_ref[pl.ds(t, 1), :]
            part = None
            for c in range(p):
                lo, hi = _unpack_words(tile_ref, s, i, c, nsel, p)
                term = (lo * xrow[:, c * LANES:(c + 1) * LANES]
                        + hi * xrow[:, (p + c) * LANES:(p + c + 1) * LANES])
                part = term if part is None else part + term
            acc = jnp.where(lane == t, jnp.sum(part, axis=1, keepdims=True), acc)
        return acc

    hpre = _peer_schedule(idx_ref, nxt_ref, tab_ref, tile_ref, bt, nsel, p, consume,
                          jnp.zeros((nsel, bt), F32))
    act_ref[...] = 0.5 * hpre * (1.0 + lax.erf(hpre * (2.0 ** -0.5))) * gate_ref[...]


def peer_u_phase(idx_t, x, gates, tab, p, bt=256):
    T, nsel = idx_t.shape
    D = x.shape[1]
    bt = _pick(T, bt)
    kern = functools.partial(_peer_u_kernel, nsel=nsel, p=p)
    return pl.pallas_call(
        kern,
        grid=(T // bt,),
        in_specs=[pl.BlockSpec((bt, nsel), lambda i: (i, 0), memory_space=pltpu.SMEM),
                  pl.BlockSpec((bt, nsel), lambda i: (jnp.minimum(i + 1, T // bt - 1), 0), memory_space=pltpu.SMEM),
                  pl.BlockSpec((bt, D), lambda i: (i, 0)),
                  pl.BlockSpec((nsel, bt), lambda i: (0, i)),
                  pl.BlockSpec(tab.shape, lambda i: (0, 0), pipeline_mode=pl.Buffered(1))],
        out_specs=pl.BlockSpec((nsel, bt), lambda i: (0, i)),
        out_shape=jax.ShapeDtypeStruct((nsel, T), F32),
        scratch_shapes=[pltpu.VMEM((2, GROUP * nsel * p, LANES), I32)],
        compiler_params=_cparams(("arbitrary",), VMEM_TABLE_LIMIT),
        name="peer_u",
    )(idx_t, idx_t, x, gates, tab)


def _peer_v_kernel(idx_ref, nxt_ref, act_ref, x_ref, tab_ref, g_ref, b_ref, o_ref, tile_ref, out_ref, *, nsel, p, alpha):
    bt = x_ref.shape[0]
    lane = lax.broadcasted_iota(I32, (nsel, bt), 1)

    def consume(t0, s, carry):
        for i in range(GROUP):
            t = t0 + i
            a = jnp.sum(jnp.where(lane == t, act_ref[...], 0.0), axis=1, keepdims=True)
            los, his = [], []
            for c in range(p):
                lo, hi = _unpack_words(tile_ref, s, i, c, nsel, p)
                los.append(jnp.sum(lo * a, axis=0, keepdims=True))
                his.append(jnp.sum(hi * a, axis=0, keepdims=True))
            out_ref[pl.ds(t, 1), :] = jnp.concatenate(los + his, axis=1)
        return carry

    _peer_schedule(idx_ref, nxt_ref, tab_ref, tile_ref, bt, nsel, p, consume, 0)
    o_ref[...] = _layer_norm(alpha * x_ref[...] + out_ref[...], g_ref[...], b_ref[...])


def peer_v_phase(idx_t, act, x, tab, g, b, p, alpha, bt=256):
    T, nsel = idx_t.shape
    D = x.shape[1]
    bt = _pick(T, bt)
    kern = functools.partial(_peer_v_kernel, nsel=nsel, p=p, alpha=alpha)
    return pl.pallas_call(
        kern,
        grid=(T // bt,),
        in_specs=[pl.BlockSpec((bt, nsel), lambda i: (i, 0), memory_space=pltpu.SMEM),
                  pl.BlockSpec((bt, nsel), lambda i: (jnp.minimum(i + 1, T // bt - 1), 0), memory_space=pltpu.SMEM),
                  pl.BlockSpec((nsel, bt), lambda i: (0, i)),
                  pl.BlockSpec((bt, D), lambda i: (i, 0)),
                  pl.BlockSpec(tab.shape, lambda i: (0, 0), pipeline_mode=pl.Buffered(1)),
                  pl.BlockSpec((1, D), lambda i: (0, 0)),
                  pl.BlockSpec((1, D), lambda i: (0, 0))],
        out_specs=pl.BlockSpec((bt, D), lambda i: (i, 0)),
        out_shape=jax.ShapeDtypeStruct((T, D), F32),
        scratch_shapes=[pltpu.VMEM((2, GROUP * nsel * p, LANES), I32), pltpu.VMEM((bt, D), F32)],
        compiler_params=_cparams(("arbitrary",), VMEM_TABLE_LIMIT),
        name="peer_v",
    )(idx_t, idx_t, act, x, tab, g, b)


def pack_table(tab):
    E, D = tab.shape
    bits = lax.bitcast_convert_type(tab.astype(BF16), jnp.uint16).astype(jnp.uint32)
    words = bits[:, :D // 2] | (bits[:, D // 2:] << 16)
    return lax.bitcast_convert_type(words, I32).reshape(E * (D // 256), LANES)


def _kv_kernel(x_ref, wkv_ref, wf_ref, wft_ref, bf_ref, bft_ref, k_ref, v_ref, k16_ref, v16_ref, lf_ref, lft_ref, *, aw):
    xb = x_ref[...].astype(BF16)
    kv = jnp.dot(xb, wkv_ref[...], preferred_element_type=F32)
    k = kv[:, :aw]
    v = kv[:, aw:]
    k_ref[...] = k
    v_ref[...] = v
    k16_ref[...] = k.astype(BF16)
    v16_ref[...] = v.astype(BF16)
    f = jnp.dot(xb, wf_ref[...], preferred_element_type=F32) + bf_ref[...]
    lf_ref[...] = jax.nn.log_sigmoid(f)
    ft = lax.dot_general(wft_ref[...], xb, NT, preferred_element_type=F32) + bft_ref[...]
    lft_ref[...] = jax.nn.log_sigmoid(ft)


def shared_kv(x, wkv, wf, wft, bf, bft, bm=512):
    T, D = x.shape
    aw = wkv.shape[1] // 2
    H = wf.shape[1]
    bm = _pick(T, bm)
    outs = [jax.ShapeDtypeStruct((T, aw), F32), jax.ShapeDtypeStruct((T, aw), F32),
            jax.ShapeDtypeStruct((T, aw), BF16), jax.ShapeDtypeStruct((T, aw), BF16),
            jax.ShapeDtypeStruct((T, H), F32), jax.ShapeDtypeStruct((H, T), F32)]
    row = lambda i: (i, 0)
    fixed = lambda i: (0, 0)
    return pl.pallas_call(
        functools.partial(_kv_kernel, aw=aw),
        grid=(T // bm,),
        in_specs=[pl.BlockSpec((bm, D), row), pl.BlockSpec((D, 2 * aw), fixed), pl.BlockSpec((D, H), fixed),
                  pl.BlockSpec((H, D), fixed), pl.BlockSpec((1, H), fixed), pl.BlockSpec((H, 1), fixed)],
        out_specs=[pl.BlockSpec((bm, aw), row), pl.BlockSpec((bm, aw), row), pl.BlockSpec((bm, aw), row),
                   pl.BlockSpec((bm, aw), row), pl.BlockSpec((bm, H), row), pl.BlockSpec((H, bm), lambda i: (0, i))],
        out_shape=outs,
        compiler_params=_cparams(("arbitrary",)),
        name="shared_kv",
    )(x, wkv, wf, wft, bf, bft)


def _cumsum_kernel(lf_ref, lft_ref, ck_ref, ckt_ref):
    L = lf_ref.shape[1]
    B = LANES
    hi = lax.Precision.HIGHEST
    r = lax.broadcasted_iota(I32, (B, B), 0)
    c = lax.broadcasted_iota(I32, (B, B), 1)
    lower = (r >= c).astype(F32)
    upper = (r <= c).astype(F32)
    H = lf_ref.shape[2]
    carry = jnp.zeros((1, H), F32)
    carry_t = jnp.zeros((H, 1), F32)
    for j in range(L // B):
        blk = jnp.dot(lower, lf_ref[0, j * B:(j + 1) * B, :], precision=hi, preferred_element_type=F32) + carry
        ck_ref[0, j * B:(j + 1) * B, :] = blk
        carry = blk[B - 1:B, :]
        blk_t = jnp.dot(lft_ref[0, :, j * B:(j + 1) * B], upper, precision=hi, preferred_element_type=F32) + carry_t
        ckt_ref[0, :, j * B:(j + 1) * B] = blk_t
        carry_t = blk_t[:, B - 1:B]


def forget_cumsum(lf, lft):
    nseq, L, H = lf.shape
    return pl.pallas_call(
        _cumsum_kernel,
        grid=(nseq,),
        in_specs=[pl.BlockSpec((1, L, H), lambda s: (s, 0, 0)), pl.BlockSpec((1, H, L), lambda s: (s, 0, 0))],
        out_specs=[pl.BlockSpec((1, L, H), lambda s: (s, 0, 0)), pl.BlockSpec((1, H, L), lambda s: (s, 0, 0))],
        out_shape=[jax.ShapeDtypeStruct((nseq, L, H), F32), jax.ShapeDtypeStruct((nseq, H, L), F32)],
        compiler_params=_cparams(("arbitrary",)),
        name="forget_cumsum",
    )(lf, lft)


FOX_SUB_ROWS = 128


def _fox_kernel(q_ref, g_ref, k_ref, v_ref, cq_ref, ckt_ref, o_ref, *, bq, bk, dh, off, scale):
    hp = pl.program_id(1)
    qi = pl.program_id(2)
    rs = FOX_SUB_ROWS if bq % FOX_SUB_ROWS == 0 else bq
    nsub = bq // rs
    lane = lax.broadcasted_iota(I32, (rs, 2 * dh), 1)
    rowi = lax.broadcasted_iota(I32, (rs, bk), 0)
    kiota = lax.broadcasted_iota(I32, (rs, bk), 1)
    nkb = (off + (qi + 1) * bq + bk - 1) // bk
    head_lane = lax.broadcasted_iota(I32, (rs, cq_ref.shape[2]), 1)
    qh, cq, qpos = [], [], []
    for sb in range(nsub):
        q2 = q_ref[sb * rs:(sb + 1) * rs, :] * scale
        cq_all = cq_ref[0, sb * rs:(sb + 1) * rs, :]
        qpos.append(off + qi * bq + sb * rs + rowi)
        for hh in range(2):
            sel = (lane >= dh) if hh else (lane < dh)
            qh.append(jnp.where(sel, q2, 0.0).astype(BF16))
            cq.append(jnp.sum(jnp.where(head_lane == 2 * hp + hh, cq_all, 0.0), axis=1, keepdims=True))

    def body(kb, carry):
        ks = pl.multiple_of(kb * bk, bk)
        kblk = k_ref[0, pl.ds(ks, bk), :]
        vblk = v_ref[0, pl.ds(ks, bk), :]
        cks = [ckt_ref[0, 2 * hp + hh, :, pl.ds(ks, bk)] for hh in range(2)]
        nch = 2 * nsub
        scores = [lax.dot_general(qh[ch], kblk, NT, preferred_element_type=F32) for ch in range(nch)]
        stats, probs = [], []
        for ch in range(nch):
            m, l, _ = carry[ch]
            causal = kiota + ks <= qpos[ch // 2]
            s = jnp.where(causal, scores[ch] + cq[ch] - cks[ch % 2], -jnp.inf)
            m_new = jnp.maximum(m, jnp.max(s, axis=1, keepdims=True))
            alpha = jnp.exp(m - m_new)
            pexp = jnp.exp(s - m_new)
            stats.append((m_new, alpha * l + jnp.sum(pexp, axis=1, keepdims=True), alpha))
            probs.append(pexp.astype(BF16))
        new = []
        for ch in range(nch):
            m_new, l_new, alpha = stats[ch]
            acc = alpha * carry[ch][2] + jnp.dot(probs[ch], vblk, preferred_element_type=F32)
            new.append((m_new, l_new, acc))
        return tuple(new)

    init = (jnp.full((rs, 1), -jnp.inf, F32), jnp.zeros((rs, 1), F32), jnp.zeros((rs, 2 * dh), F32))
    fin = lax.fori_loop(0, nkb, body, (init,) * (2 * nsub))
    for sb in range(nsub):
        (_, l0, a0), (_, l1, a1) = fin[2 * sb], fin[2 * sb + 1]
        o = jnp.where(lane < dh, a0 / l0, a1 / l1)
        o_ref[sb * rs:(sb + 1) * rs, :] = o * jax.nn.sigmoid(g_ref[sb * rs:(sb + 1) * rs, :])


def fox_attention(qg, k16, v16, ck, ckt, tok0, nseq, L, Lk, dh, bq, bk):
    aw = k16.shape[2]
    Lkp = k16.shape[1]
    H = ck.shape[2]
    off = Lk - L
    nq = L // bq
    tb0 = tok0 // bq
    npair = aw // (2 * dh)
    kern = functools.partial(_fox_kernel, bq=bq, bk=bk, dh=dh, off=off, scale=dh ** -0.5)
    return pl.pallas_call(
        kern,
        grid=(nseq, npair, nq),
        in_specs=[pl.BlockSpec((bq, 2 * dh), lambda s, h, i: (tb0 + s * nq + i, h)),
                  pl.BlockSpec((bq, 2 * dh), lambda s, h, i: (tb0 + s * nq + i, npair + h)),
                  pl.BlockSpec((1, Lkp, 2 * dh), lambda s, h, i: (s, 0, h)),
                  pl.BlockSpec((1, Lkp, 2 * dh), lambda s, h, i: (s, 0, h)),
                  pl.BlockSpec((1, bq, H), lambda s, h, i: (s, off // bq + i, 0)),
                  pl.BlockSpec((1, H, 1, Lkp), lambda s, h, i: (s, 0, 0, 0))],
        out_specs=pl.BlockSpec((bq, 2 * dh), lambda s, h, i: (s * nq + i, h)),
        out_shape=jax.ShapeDtypeStruct((nseq * L, aw), F32),
        compiler_params=_cparams(("arbitrary", "arbitrary", "arbitrary")),
        name="fox_attention",
    )(qg, qg, k16, v16, ck, ckt.reshape(nseq, H, 1, Lkp))


def _pad_lanes(v, n=LANES):
    v = v.reshape(1, -1).astype(F32)
    return jnp.pad(v, ((0, 0), (0, n - v.shape[1])))


def _round_up(n, m):
    return (n + m - 1) // m * m


def kernel(x_prompt, x_sample, state_ssm, state_conv, cache_k, cache_v, cache_logf, a_w_in, a_conv_w, a_conv_b, a_dt_bias, a_A_log, a_D, a_norm_w, a_w_out, kv_w, kv_b_f, b_w_qg, b_w_o, peer_w_q, peer_subkeys, peer_u, peer_v, ln_g, ln_b):
    Bp, S, D = x_prompt.shape
    Bs, Ss, _ = x_sample.shape
    depth = ln_g.shape[0]
    n_a = a_w_in.shape[0]
    H = a_dt_bias.shape[1]
    DI = a_w_out.shape[1]
    P = DI // H
    N = state_ssm.shape[-1]
    C = a_conv_w.shape[2]
    G = (C - DI) // (2 * N)
    past = cache_k.shape[1]
    AH, dh = cache_k.shape[2], cache_k.shape[3]
    AW = AH * dh
    alpha = (2.0 * depth) ** 0.25
    Tp, Ts = Bp * S, Bs * Ss
    T = Tp + Ts
    nseq = Bp + Bs
    assert S % CHUNK == 0 and Ss % CHUNK == 0 and H <= LANES and D % 256 == 0 and LANES % P == 0
    per = LANES // P
    assert (H // G) % per == 0

    seq_np = np.concatenate([np.repeat(np.arange(Bp), S // CHUNK), Bp + np.repeat(np.arange(Bs), Ss // CHUNK)])
    first_np = np.concatenate([[1], (seq_np[1:] != seq_np[:-1]).astype(np.int64)])
    seq = jnp.asarray(seq_np, I32)
    first = jnp.asarray(first_np, I32)

    x = jnp.concatenate([x_prompt.reshape(Tp, D), x_sample.reshape(Ts, D)], axis=0)
    rows_per_expert = D // 256
    new_conv, new_ssm = [], []
    k_f32 = v_f32 = lf = None

    for i in range(depth):
        g1, b1 = ln_g[i, 0].reshape(1, D), ln_b[i, 0].reshape(1, D)
        g2, b2 = ln_g[i, 1].reshape(1, D), ln_b[i, 1].reshape(1, D)
        if i < n_a:
            w_in = a_w_in[i].astype(BF16)
            z = matmul(x, w_in[:, :DI])
            xbc_raw = matmul(x, w_in[:, DI:DI + C])
            w_dt = jnp.pad(w_in[:, DI + C:], ((0, 0), (0, LANES - H)))
            dt_raw = matmul(x, w_dt)
            conv0 = jnp.concatenate([jnp.zeros((Bp,) + state_conv.shape[2:], F32), state_conv[i]], axis=0)
            ssm0 =jnp.concatenate([jnp.zeros((Bp, H, P, N), F32), state_ssm[i]], axis=0)
            ssm0_t = ssm0.reshape(nseq, H // per, per, P, N).transpose(0, 1, 4, 2, 3).reshape(nseq, H // per, N, LANES)
            y, ssm_new, conv_new = ssd_scan(xbc_raw, conv0, a_conv_w[i], a_conv_b[i].reshape(1, C), dt_raw,
                                            _pad_lanes(a_dt_bias[i]), _pad_lanes(a_A_log[i]),
                                            jnp.repeat(a_D[i].astype(F32), P).reshape(1, DI), ssm0_t, first, seq,
                                            H, P, N, G)
            new_conv.append(conv_new)
            new_ssm.append(ssm_new.reshape(nseq, H // per, N, per, P).transpose(0, 1, 3, 4, 2).reshape(nseq, H, P, N))
            x = mamba_out(y, z, x, a_norm_w[i].reshape(1, DI), a_w_out[i].astype(BF16), g1, b1, G, alpha)
        else:
            j = i - n_a
            qg = matmul(x, b_w_qg[j].astype(BF16))
            o_p = fox_attention(qg, kp16, vp16, ck_p, ckt_p, 0, Bp, S, S, dh, bq_p, bk_p)
            o_s = fox_attention(qg, ks16, vs16, ck_s, ckt_s, Tp, Bs, Ss, past + Ss, dh, bq_s, bk_s)
            x = matmul_res_ln(jnp.concatenate([o_p, o_s], axis=0), b_w_o[j].astype(BF16), x, g1, b1, alpha)

        idx, gates = peer_topk(x, peer_w_q[i].astype(BF16), peer_subkeys[i].astype(BF16), rows_per_expert)
        idx_t = idx.T
        act = peer_u_phase(idx_t, x, gates, pack_table(peer_u[i]), rows_per_expert)
        x = peer_v_phase(idx_t, act, x, pack_table(peer_v[i]), g2, b2, rows_per_expert, alpha)

        if i == n_a - 1:
            wkv = kv_w[:, :2 * AW].astype(BF16)
            wf = kv_w[:, 2 * AW:].astype(BF16)
            k_f32, v_f32, k16, v16, lf, lft = shared_kv(x, wkv, wf, wf.T, kv_b_f.reshape(1, AH), kv_b_f.reshape(AH, 1))
            bq_p = _pick(S, 2 * FOX_SUB_ROWS)
            bk_p = _pick(S, 512)
            kp16 = k16[:Tp].reshape(Bp, S, AW)
            vp16 = v16[:Tp].reshape(Bp, S, AW)
            Sp = _round_up(S, LANES)
            lf_p = jnp.pad(lf[:Tp].reshape(Bp, S, AH), ((0, 0), (0, Sp - S), (0, 0)))
            lft_p = jnp.pad(jnp.swapaxes(lft[:, :Tp].reshape(AH, Bp, S), 0, 1), ((0, 0), (0, 0), (0, Sp - S)))
            ck_p, ckt_p = forget_cumsum(lf_p, lft_p)
            Lk = past + Ss
            bq_s = _pick(Ss, 128)
            Lkp = _round_up(Lk, LANES)
            bk_s = max(b for b in range(LANES, 4 * LANES + 1, LANES) if Lkp % b == 0)
            padk = ((0, 0), (0, Lkp - Lk), (0, 0))
            ks16 = jnp.pad(jnp.concatenate([cache_k.reshape(Bs, past, AW).astype(BF16), k16[Tp:].reshape(Bs, Ss, AW)], axis=1), padk)
            vs16 = jnp.pad(jnp.concatenate([cache_v.reshape(Bs, past, AW).astype(BF16), v16[Tp:].reshape(Bs, Ss, AW)], axis=1), padk)
            lf_s = jnp.pad(jnp.concatenate([cache_logf.astype(F32), lf[Tp:].reshape(Bs, Ss, AH)], axis=1), padk)
            lft_new = jnp.swapaxes(lft[:, Tp:].reshape(AH, Bs, Ss), 0, 1)
            lft_s = jnp.pad(jnp.concatenate([jnp.swapaxes(cache_logf.astype(F32), 1, 2), lft_new], axis=2),
                            ((0, 0), (0, 0), (0, Lkp - Lk)))
            ck_s, ckt_s = forget_cumsum(lf_s, lft_s)

    conv_all = jnp.stack(new_conv)
    ssm_all = jnp.stack(new_ssm)
    return (x[:Tp].reshape(Bp, S, D), x[Tp:].reshape(Bs, Ss, D),
            ssm_all[:, :Bp], conv_all[:, :Bp],
            k_f32[:Tp].reshape(Bp, S, AH, dh), v_f32[:Tp].reshape(Bp, S, AH, dh), lf[:Tp].reshape(Bp, S, AH),
            ssm_all[:, Bp:], conv_all[:, Bp:],
            k_f32[Tp:].reshape(Bs, Ss, AH, dh), v_f32[Tp:].reshape(Bs, Ss, AH, dh), lf[Tp:].reshape(Bs, Ss, AH))
```

```python
import functools
import math

import numpy as np
import jax
import jax.numpy as jnp
from jax import lax
from jax.experimental import pallas as pl
from jax.experimental.pallas import tpu as pltpu

F32 = jnp.float32
BF16 = jnp.bfloat16
I32 = jnp.int32

CHUNK = 64
PEER_TOPK = 16
RMS_EPS = 1e-5
LN_EPS = 1e-5
LANES = 128
VMEM_TABLE_LIMIT = 56 * 1024 * 1024
NT = (((1,), (1,)), ((), ()))


def _pick(n, pref):
    if n <= pref:
        return n
    for b in range(pref, 7, -1):
        if n % b == 0 and b % 8 == 0:
            return b
    return n


def _layer_norm(v, g, b):
    mu = jnp.mean(v, -1, keepdims=True)
    d = v - mu
    var = jnp.mean(d * d, -1, keepdims=True)
    return d * lax.rsqrt(var + LN_EPS) * g + b


def _cparams(sem, vmem=None):
    kw = dict(dimension_semantics=sem)
    if vmem is not None:
        kw["vmem_limit_bytes"] = vmem
    return pltpu.CompilerParams(**kw)


def _mm_kernel(x_ref, w_ref, *o_refs, splits):
    xb = x_ref[...].astype(BF16)
    off = 0
    for o_ref, n in zip(o_refs, splits):
        o_ref[...] = jnp.dot(xb, w_ref[:, off:off + n], preferred_element_type=F32)
        off += n


def matmul(x, w, splits, bm=512):
    T, K = x.shape
    assert w.shape[1] == sum(splits) and all(n % LANES == 0 for n in splits)
    bm = _pick(T, bm)
    return pl.pallas_call(
        functools.partial(_mm_kernel, splits=tuple(splits)),
        grid=(T // bm,),
        in_specs=[pl.BlockSpec((bm, K), lambda i: (i, 0)),
                  pl.BlockSpec(w.shape, lambda i: (0, 0), pipeline_mode=pl.Buffered(1))],
        out_specs=[pl.BlockSpec((bm, n), lambda i: (i, 0)) for n in splits],
        out_shape=[jax.ShapeDtypeStruct((T, n), F32) for n in splits],
        compiler_params=_cparams(("arbitrary",), VMEM_TABLE_LIMIT),
        name="matmul",
    )(x, w)


def _mm_res_ln_kernel(a_ref, w_ref, x_ref, g_ref, b_ref, o_ref, *, alpha):
    y = jnp.dot(a_ref[...].astype(BF16), w_ref[...], preferred_element_type=F32)
    o_ref[...] = _layer_norm(alpha * x_ref[...] + y, g_ref[...], b_ref[...])


def matmul_res_ln(a, w, x, g, b, alpha, bm=512):
    T, K = a.shape
    D = w.shape[1]
    bm = _pick(T, bm)
    return pl.pallas_call(
        functools.partial(_mm_res_ln_kernel, alpha=alpha),
        grid=(T // bm,),
        in_specs=[pl.BlockSpec((bm, K), lambda i: (i, 0)),
                  pl.BlockSpec((K, D), lambda i: (0, 0)),
                  pl.BlockSpec((bm, D), lambda i: (i, 0)),
                  pl.BlockSpec((1, D), lambda i: (0, 0)),
                  pl.BlockSpec((1, D), lambda i: (0, 0))],
        out_specs=pl.BlockSpec((bm, D), lambda i: (i, 0)),
        out_shape=jax.ShapeDtypeStruct((T, D), F32),
        compiler_params=_cparams(("arbitrary",)),
        name="matmul_res_ln",
    )(a, w, x, g, b)


def _ssd_kernel(first_ref, seq_ref, u_ref, cinit_ref, cw_ref, cb_ref, dt_ref, bias_ref, alog_ref, dsk_ref, init_ref,
                y_ref, fin_ref, cst_ref, st_ref, ext_ref, xbc_ref, *, H, P, N, G, dconv):
    c = pl.program_id(0)
    nprev = dconv - 1

    @pl.when(first_ref[c] == 1)
    def _():
        st_ref[...] = init_ref[0]
        ext_ref[8 - nprev:8, :] = cinit_ref[0]

    u = u_ref[...]
    ext_ref[8:8 + CHUNK, :] = u
    cw = cw_ref[...]
    acc = cb_ref[...] + u * cw[dconv - 1:dconv]
    for k in range(nprev):
        acc = acc + ext_ref[8 - nprev + k:8 - nprev + k + CHUNK, :] * cw[k:k + 1]
    xbc_ref[...] = acc * jax.nn.sigmoid(acc)
    tail = ext_ref[8 + CHUNK - nprev:8 + CHUNK, :]
    cst_ref[0] = tail
    ext_ref[8 - nprev:8, :] = tail

    Q = CHUNK
    DI = H * P
    R = H // G
    per = LANES // P
    hi = lax.Precision.HIGHEST
    dt = jax.nn.softplus(dt_ref[...] + bias_ref[...])
    a = dt * (-jnp.exp(alog_ref[...]))
    row = lax.broadcasted_iota(I32, (Q, Q), 0)
    col = lax.broadcasted_iota(I32, (Q, Q), 1)
    tril = row >= col
    lane_q = lax.broadcasted_iota(I32, (Q, LANES), 1)
    a_cs = jnp.dot(tril.astype(F32), a, precision=hi, preferred_element_type=F32)
    eye_l = (lax.broadcasted_iota(I32, (LANES, LANES), 0) == lax.broadcasted_iota(I32, (LANES, LANES), 1)).astype(F32)
    a_cs_t = lax.dot_general(eye_l, a_cs, NT, precision=hi, preferred_element_type=F32)
    last = a_cs[Q - 1:Q, :]
    exp_cs = jnp.exp(a_cs)
    decay = jnp.exp(last - a_cs)
    chunk_decay = jnp.exp(last)
    dsk = dsk_ref[...]
    eye_n = (lax.broadcasted_iota(I32, (N, N), 0) == lax.broadcasted_iota(I32, (N, N), 1)).astype(BF16)

    for g in range(G):
        b16 = xbc_ref[:, DI + g * N:DI + (g + 1) * N].astype(BF16)
        c16 = xbc_ref[:, DI + (G + g) * N:DI + (G + g + 1) * N].astype(BF16)
        cb = lax.dot_general(c16, b16, NT, preferred_element_type=F32)
        b16_t = lax.dot_general(eye_n, b16, NT, preferred_element_type=F32).astype(BF16)
        slabs = [(g * R + r0) // per for r0 in range(0, R, per)]
        xs_l, xw16, st_l, ecs_l, cdec_l, xm16, m16_l = [], [], [], [], [], [], []
        for sl in slabs:
            h0 = sl * per
            xs_w = xbc_ref[:, sl * LANES:(sl + 1) * LANES]

            def wide(v, h0=h0):
                out = v[:, h0:h0 + 1]
                for hh in range(1, per):
                    out = jnp.where(lane_q >= hh * P, v[:, h0 + hh:h0 + hh + 1], out)
                return out

            xdt_w = xs_w * wide(dt)
            xs_l.append(xs_w)
            xw16.append((xdt_w * wide(decay)).astype(BF16))
            ecs_l.append(wide(exp_cs))
            cdec_l.append(wide(chunk_decay)[0:1])
            st_l.append(st_ref[sl])
            for hh in range(per):
                h = h0 + hh
                seg = a_cs[:, h:h + 1] - a_cs_t[h:h + 1, :]
                lmat = jnp.where(tril, jnp.exp(jnp.where(tril, seg, 0.0)), 0.0)
                m16_l.append((cb * lmat).astype(BF16))
                own = (lane_q >= hh * P) & (lane_q < (hh + 1) * P)
                xm16.append(jnp.where(own, xdt_w, 0.0).astype(BF16))
        ns = len(slabs)
        y_off = [jnp.dot(c16, st_l[i].astype(BF16), preferred_element_type=F32) for i in range(ns)]
        y_dia = [jnp.dot(m16_l[i], xm16[i], preferred_element_type=F32) for i in range(ns * per)]
        s_new = [jnp.dot(b16_t, xw16[i], preferred_element_type=F32) for i in range(ns)]
        for i, sl in enumerate(slabs):
            y = y_off[i] * ecs_l[i] + xs_l[i] * dsk[:, sl * LANES:(sl + 1) * LANES]
            for hh in range(per):
                y = y + y_dia[i * per + hh]
            y_ref[:, sl * LANES:(sl + 1) * LANES] = y
            st_ref[sl] = st_l[i] * cdec_l[i] + s_new[i]
    fin_ref[0] = st_ref[...]


def ssd_scan(u, conv_init, conv_w, conv_b, dt_raw, dt_bias, a_log, d_skip, init_t, first, seq, H, P, N, G):
    T, C = u.shape
    nseq, nprev, _ = conv_init.shape
    dconv = nprev + 1
    DI = H * P
    NS = DI // LANES
    kern = functools.partial(_ssd_kernel, H=H, P=P, N=N, G=G, dconv=dconv)
    grid_spec = pltpu.PrefetchScalarGridSpec(
        num_scalar_prefetch=2,
        grid=(T // CHUNK,),
        in_specs=[pl.BlockSpec((CHUNK, C), lambda c, f, s: (c, 0)),
                  pl.BlockSpec((1, nprev, C), lambda c, f, s: (s[c], 0, 0)),
                  pl.BlockSpec((dconv, C), lambda c, f, s: (0, 0)),
                  pl.BlockSpec((1, C), lambda c, f, s: (0, 0)),
                  pl.BlockSpec((CHUNK, LANES), lambda c, f, s: (c, 0)),
                  pl.BlockSpec((1, LANES), lambda c, f, s: (0, 0)),
                  pl.BlockSpec((1, LANES), lambda c, f, s: (0, 0)),
                  pl.BlockSpec((1, DI), lambda c, f, s: (0, 0)),
                  pl.BlockSpec((1, NS, N, LANES), lambda c, f, s: (s[c], 0, 0, 0))],
        out_specs=[pl.BlockSpec((CHUNK, DI), lambda c, f, s: (c, 0)),
                   pl.BlockSpec((1, NS, N, LANES), lambda c, f, s: (s[c], 0, 0, 0)),
                   pl.BlockSpec((1, nprev, C), lambda c, f, s: (s[c], 0, 0))],
        scratch_shapes=[pltpu.VMEM((NS, N, LANES), F32), pltpu.VMEM((8 + CHUNK, C), F32), pltpu.VMEM((CHUNK, C), F32)],
    )
    return pl.pallas_call(
        kern,
        grid_spec=grid_spec,
        out_shape=[jax.ShapeDtypeStruct((T, DI), F32), jax.ShapeDtypeStruct((nseq, NS, N, LANES), F32),
                   jax.ShapeDtypeStruct((nseq, nprev, C), F32)],
        compiler_params=_cparams(("arbitrary",)),
        name="ssd_scan",
    )(first, seq, u, conv_init, conv_w, conv_b, dt_raw, dt_bias, a_log, d_skip, init_t)


def _mamba_out_kernel(y_ref, z_ref, x_ref, nw_ref, w_ref, g_ref, b_ref, o_ref, *, groups, alpha):
    z = z_ref[...]
    h = y_ref[...] * (z * jax.nn.sigmoid(z))
    gs = h.shape[1] // groups
    parts = []
    for gi in range(groups):
        hg = h[:, gi * gs:(gi + 1) * gs]
        parts.append(hg * lax.rsqrt(jnp.mean(hg * hg, -1, keepdims=True) + RMS_EPS))
    hn = jnp.concatenate(parts, axis=1) * nw_ref[...]
    out = jnp.dot(hn.astype(BF16), w_ref[...], preferred_element_type=F32)
    o_ref[...] = _layer_norm(alpha * x_ref[...] + out, g_ref[...], b_ref[...])


def mamba_out(y, z, x, norm_w, w_out, g, b, groups, alpha, bm=256):
    T, DI = y.shape
    D = x.shape[1]
    bm = _pick(T, bm)
    return pl.pallas_call(
        functools.partial(_mamba_out_kernel, groups=groups, alpha=alpha),
        grid=(T // bm,),
        in_specs=[pl.BlockSpec((bm, DI), lambda i: (i, 0)),
                  pl.BlockSpec((bm, DI), lambda i: (i, 0)),
                  pl.BlockSpec((bm, D), lambda i: (i, 0)),
                  pl.BlockSpec((1, DI), lambda i: (0, 0)),
                  pl.BlockSpec((DI, D), lambda i: (0, 0)),
                  pl.BlockSpec((1, D), lambda i: (0, 0)),
                  pl.BlockSpec((1, D), lambda i: (0, 0))],
        out_specs=pl.BlockSpec((bm, D), lambda i: (i, 0)),
        out_shape=jax.ShapeDtypeStruct((T, D), F32),
        compiler_params=_cparams(("arbitrary",)),
        name="mamba_out",
    )(y, z, x, norm_w, w_out, g, b)


def _peer_topk_kernel(x_ref, wq_ref, sk_ref, idx_ref, gate_ref, *, nk, half, rows_per_expert):
    q = jnp.dot(x_ref[...].astype(BF16), wq_ref[...], preferred_element_type=F32)
    bt = q.shape[0]
    K = PEER_TOPK
    cw = min(bt, LANES)
    neg = jnp.float32(-jnp.inf)
    iota_k = lax.broadcasted_iota(I32, (nk, cw), 0).astype(F32)

    def fiota(rows):
        return lax.broadcasted_iota(I32, (rows, cw), 0).astype(F32)

    nexp = nk * nk
    assert K * K * nexp < 2 ** 24
    chunks = range(bt // cw)
    cands, keys = [], []
    chains = [(c, j) for c in chunks for j in range(2)]
    ss, v_rows, i_rows = {}, {}, {}
    for c, j in chains:
        qc = q[c * cw:(c + 1) * cw, j * half:(j + 1) * half].astype(BF16)
        ss[c, j] = lax.dot_general(sk_ref[j], qc, NT, preferred_element_type=F32)
        v_rows[c, j], i_rows[c, j] = [], []
    for _ in range(K):
        for ch in chains:
            m = jnp.max(ss[ch], axis=0, keepdims=True)
            am = jnp.min(jnp.where(ss[ch] == m, iota_k, float(nk)), axis=0, keepdims=True)
            v_rows[ch].append(m)
            i_rows[ch].append(am)
            ss[ch] = jnp.where(iota_k == am, neg, ss[ch])
    for c in chunks:
        sv = [jnp.concatenate(v_rows[c, j], axis=0) for j in range(2)]
        si = [jnp.concatenate(i_rows[c, j], axis=0) for j in range(2)]
        vals, kys = [], []
        a = 0
        while a < K and K // (a + 1) >= 2:
            n_a = K // (a + 1)
            rows = _round_up(n_a, 8)
            b_pos = fiota(rows)
            vals.append(jnp.where(b_pos < n_a, sv[0][a:a + 1] + sv[1][:rows], neg))
            kys.append((b_pos + float(a * K)) * float(nexp) + (si[0][a:a + 1] * float(nk) + si[1][:rows]))
            a += 1
        if a < K:
            vals.append(sv[0][a:K] + sv[1][0:1])
            kys.append((fiota(K - a) + float(a)) * float(K * nexp) + (si[0][a:K] * float(nk) + si[1][0:1]))
        cands.append(jnp.concatenate(vals, axis=0))
        keys.append(jnp.concatenate(kys, axis=0))
    best = [[] for _ in chunks]
    kmin = [[] for _ in chunks]
    for _ in range(K):
        for c in chunks:
            m = jnp.max(cands[c], axis=0, keepdims=True)
            kk = jnp.min(jnp.where(cands[c] == m, keys[c], float(2 ** 24)), axis=0, keepdims=True)
            best[c].append(m)
            kmin[c].append(kk)
            cands[c] = jnp.where(keys[c] == kk, neg, cands[c])
    for c in chunks:
        bestv = jnp.concatenate(best[c], axis=0)
        pr = jnp.exp(bestv - bestv[0:1])
        gate_ref[:, c * cw:(c + 1) * cw] = pr / jnp.sum(pr, axis=0, keepdims=True)
        ki = jnp.concatenate(kmin[c], axis=0).astype(I32)
        eid = (ki & (nexp - 1)) if nexp & (nexp - 1) == 0 else lax.rem(ki, nexp)
        idx_ref[:, c * cw:(c + 1) * cw] = eid * rows_per_expert


def peer_topk(x, wq, sk, rows_per_expert, bt=256):
    T, D = x.shape
    nk, half = sk.shape[1], sk.shape[2]
    nh = wq.shape[1] // (2 * half)
    K = PEER_TOPK
    bt = _pick(T, bt)
    kern = functools.partial(_peer_topk_kernel, nk=nk, half=half, rows_per_expert=rows_per_expert)
    return pl.pallas_call(
        kern,
        grid=(T // bt, nh),
        in_specs=[pl.BlockSpec((bt, D), lambda i, h: (i, 0)),
                  pl.BlockSpec((D, 2 * half), lambda i, h: (0, h)),
                  pl.BlockSpec((2, nk, half), lambda i, h: (0, 0, 0))],
        out_specs=[pl.BlockSpec((K, bt), lambda i, h: (h, i)),
                   pl.BlockSpec((K, bt), lambda i, h: (h, i))],
        out_shape=[jax.ShapeDtypeStruct((nh * K, T), I32), jax.ShapeDtypeStruct((nh * K, T), F32)],
        compiler_params=_cparams(("arbitrary", "arbitrary")),
        name="peer_topk",
    )(x, wq, sk)


GROUP = 8


def _gather_rows(idx_ref, t0, tab_ref, tile_ref, nsel, p):
    rows = [idx_ref.at[t0 + i] for i in range(GROUP)]
    for k in range(nsel):
        for i in range(GROUP):
            e = pl.multiple_of(rows[i][k], p)
            tile_ref[pl.ds((i * nsel + k) * p, p), :] = tab_ref[pl.ds(e, p), :]


def _unpack_words(tile_ref, s, i, c, nsel, p):
    w = tile_ref[s, pl.ds(i * nsel * p + c, nsel, stride=p), :] if p > 1 else tile_ref[s, pl.ds(i * nsel, nsel), :]
    lo = lax.bitcast_convert_type(w << 16, F32)
    hi = lax.bitcast_convert_type(w & jnp.int32(-65536), F32)
    return lo, hi


def _peer_schedule(idx_ref, nxt_ref, tab_ref, tile_ref, bt, nsel, p, consume, carry):
    @pl.when(pl.program_id(0) == 0)
    def _():
        _gather_rows(idx_ref, 0, tab_ref, tile_ref.at[0], nsel, p)

    def body(j, c):
        for s in range(2):
            t0 = (2 * j + s) * GROUP
            _gather_rows(idx_ref, t0 + GROUP, tab_ref, tile_ref.at[1 - s], nsel, p)
            c = consume(t0, s, c)
        return c

    ntrip = bt // (2 * GROUP)
    carry = lax.fori_loop(0, ntrip - 1, body, carry)
    t0 = (ntrip - 1) * 2 * GROUP
    _gather_rows(idx_ref, t0 + GROUP, tab_ref, tile_ref.at[1], nsel, p)
    carry = consume(t0, 0, carry)
    _gather_rows(nxt_ref, 0, tab_ref, tile_ref.at[0], nsel, p)
    return consume(t0 + GROUP, 1, carry)


def _peer_u_kernel(idx_ref, nxt_ref, x_ref, gate_ref, tab_ref, act_ref, tile_ref, *, nsel, p):
    bt = x_ref.shape[0]
    lane = lax.broadcasted_iota(I32, (nsel, bt), 1)

    def consume(t0, s, acc):
        for i in range(GROUP):
            t = t0 + i
            xrow = x_ref[pl.ds(t, 1), :]
            part = None
            for c in range(p):
                lo, hi = _unpack_words(tile_ref, s, i, c, nsel, p)
                term = (lo * xrow[:, c * LANES:(c + 1) * LANES]
                        + hi * xrow[:, (p + c) * LANES:(p + c + 1) * LANES])
                part = term if part is None else part + term
            acc = jnp.where(lane == t, jnp.sum(part, axis=1, keepdims=True), acc)
        return acc

    hpre = _peer_schedule(idx_ref, nxt_ref, tab_ref, tile_ref, bt, nsel, p, consume,
                          jnp.zeros((nsel, bt), F32))
    act_ref[...] = 0.5 * hpre * (1.0 + lax.erf(hpre * (2.0 ** -0.5))) * gate_ref[...]


def peer_u_phase(idx_t, x, gates, tab, p, bt=256):
    T, nsel = idx_t.shape
    D = x.shape[1]
    bt = _pick(T, bt)
    kern = functools.partial(_peer_u_kernel, nsel=nsel, p=p)
    return pl.pallas_call(
        kern,
        grid=(T // bt,),
        in_specs=[pl.BlockSpec((bt, nsel), lambda i: (i, 0), memory_space=pltpu.SMEM),
                  pl.BlockSpec((bt, nsel), lambda i: (jnp.minimum(i + 1, T // bt - 1), 0), memory_space=pltpu.SMEM),
                  pl.BlockSpec((bt, D), lambda i: (i, 0)),
                  pl.BlockSpec((nsel, bt), lambda i: (0, i)),
                  pl.BlockSpec(tab.shape, lambda i: (0, 0), pipeline_mode=pl.Buffered(1))],
        out_specs=pl.BlockSpec((nsel, bt), lambda i: (0, i)),
        out_shape=jax.ShapeDtypeStruct((nsel, T), F32),
        scratch_shapes=[pltpu.VMEM((2, GROUP * nsel * p, LANES), I32)],
        compiler_params=_cparams(("arbitrary",), VMEM_TABLE_LIMIT),
        name="peer_u",
    )(idx_t, idx_t, x, gates, tab)


def _peer_v_kernel(idx_ref, nxt_ref, act_ref, x_ref, tab_ref, g_ref, b_ref, o_ref, tile_ref, out_ref, *, nsel, p, alpha):
    bt = x_ref.shape[0]
    lane = lax.broadcasted_iota(I32, (nsel, bt), 1)

    def consume(t0, s, carry):
        for i in range(GROUP):
            t = t0 + i
            a = jnp.sum(jnp.where(lane == t, act_ref[...], 0.0), axis=1, keepdims=True)
            los, his = [], []
            for c in range(p):
                lo, hi = _unpack_words(tile_ref, s, i, c, nsel, p)
                los.append(jnp.sum(lo * a, axis=0, keepdims=True))
                his.append(jnp.sum(hi * a, axis=0, keepdims=True))
            out_ref[pl.ds(t, 1), :] = jnp.concatenate(los + his, axis=1)
        return carry

    _peer_schedule(idx_ref, nxt_ref, tab_ref, tile_ref, bt, nsel, p, consume, 0)
    o_ref[...] = _layer_norm(alpha * x_ref[...] + out_ref[...], g_ref[...], b_ref[...])


def peer_v_phase(idx_t, act, x, tab, g, b, p, alpha, bt=256):
    T, nsel = idx_t.shape
    D = x.shape[1]
    bt = _pick(T, bt)
    kern = functools.partial(_peer_v_kernel, nsel=nsel, p=p, alpha=alpha)
    return pl.pallas_call(
        kern,
        grid=(T // bt,),
        in_specs=[pl.BlockSpec((bt, nsel), lambda i: (i, 0), memory_space=pltpu.SMEM),
                  pl.BlockSpec((bt, nsel), lambda i: (jnp.minimum(i + 1, T // bt - 1), 0), memory_space=pltpu.SMEM),
                  pl.BlockSpec((nsel, bt), lambda i: (0, i)),
                  pl.BlockSpec((bt, D), lambda i: (i, 0)),
                  pl.BlockSpec(tab.shape, lambda i: (0, 0), pipeline_mode=pl.Buffered(1)),
                  pl.BlockSpec((1, D), lambda i: (0, 0)),
                  pl.BlockSpec((1, D), lambda i: (0, 0))],
        out_specs=pl.BlockSpec((bt, D), lambda i: (i, 0)),
        out_shape=jax.ShapeDtypeStruct((T, D), F32),
        scratch_shapes=[pltpu.VMEM((2, GROUP * nsel * p, LANES), I32), pltpu.VMEM((bt, D), F32)],
        compiler_params=_cparams(("arbitrary",), VMEM_TABLE_LIMIT),
        name="peer_v",
    )(idx_t, idx_t, act, x, tab, g, b)


def pack_table(tab):
    E, D = tab.shape
    bits = lax.bitcast_convert_type(tab.astype(BF16), jnp.uint16).astype(jnp.uint32)
    words = bits[:, :D // 2] | (bits[:, D // 2:] << 16)
    return lax.bitcast_convert_type(words, I32).reshape(E * (D // 256), LANES)


def _kv_kernel(x_ref, wkv_ref, wf_ref, wft_ref, bf_ref, bft_ref, k_ref, v_ref, k16_ref, v16_ref, lf_ref, lft_ref, *, aw):
    xb = x_ref[...].astype(BF16)
    kv = jnp.dot(xb, wkv_ref[...], preferred_element_type=F32)
    k = kv[:, :aw]
    v = kv[:, aw:]
    k_ref[...] = k
    v_ref[...] = v
    k16_ref[...] = k.astype(BF16)
    v16_ref[...] = v.astype(BF16)
    f = jnp.dot(xb, wf_ref[...], preferred_element_type=F32) + bf_ref[...]
    lf_ref[...] = jax.nn.log_sigmoid(f)
    ft = lax.dot_general(wft_ref[...], xb, NT, preferred_element_type=F32) + bft_ref[...]
    lft_ref[...] = jax.nn.log_sigmoid(ft)


def shared_kv(x, wkv, wf, wft, bf, bft, bm=512):
    T, D = x.shape
    aw = wkv.shape[1] // 2
    H = wf.shape[1]
    bm = _pick(T, bm)
    outs = [jax.ShapeDtypeStruct((T, aw), F32), jax.ShapeDtypeStruct((T, aw), F32),
            jax.ShapeDtypeStruct((T, aw), BF16), jax.ShapeDtypeStruct((T, aw), BF16),
            jax.ShapeDtypeStruct((T, H), F32), jax.ShapeDtypeStruct((H, T), F32)]
    row = lambda i: (i, 0)
    fixed = lambda i: (0, 0)
    return pl.pallas_call(
        functools.partial(_kv_kernel, aw=aw),
        grid=(T // bm,),
        in_specs=[pl.BlockSpec((bm, D), row), pl.BlockSpec((D, 2 * aw), fixed), pl.BlockSpec((D, H), fixed),
                  pl.BlockSpec((H, D), fixed), pl.BlockSpec((1, H), fixed), pl.BlockSpec((H, 1), fixed)],
        out_specs=[pl.BlockSpec((bm, aw), row), pl.BlockSpec((bm, aw), row), pl.BlockSpec((bm, aw), row),
                   pl.BlockSpec((bm, aw), row), pl.BlockSpec((bm, H), row), pl.BlockSpec((H, bm), lambda i: (0, i))],
        out_shape=outs,
        compiler_params=_cparams(("arbitrary",)),
        name="shared_kv",
    )(x, wkv, wf, wft, bf, bft)


def _cumsum_kernel(lf_ref, lft_ref, ck_ref, ckt_ref):
    L = lf_ref.shape[1]
    B = LANES
    hi = lax.Precision.HIGHEST
    r = lax.broadcasted_iota(I32, (B, B), 0)
    c = lax.broadcasted_iota(I32, (B, B), 1)
    lower = (r >= c).astype(F32)
    upper = (r <= c).astype(F32)
    H = lf_ref.shape[2]
    carry = jnp.zeros((1, H), F32)
    carry_t = jnp.zeros((H, 1), F32)
    for j in range(L // B):
        blk = jnp.dot(lower, lf_ref[0, j * B:(j + 1) * B, :], precision=hi, preferred_element_type=F32) + carry
        ck_ref[0, j * B:(j + 1) * B, :] = blk
        carry = blk[B - 1:B, :]
        blk_t = jnp.dot(lft_ref[0, :, j * B:(j + 1) * B], upper, precision=hi, preferred_element_type=F32) + carry_t
        ckt_ref[0, :, j * B:(j + 1) * B] = blk_t
        carry_t = blk_t[:, B - 1:B]


def forget_cumsum(lf, lft):
    nseq, L, H = lf.shape
    return pl.pallas_call(
        _cumsum_kernel,
        grid=(nseq,),
        in_specs=[pl.BlockSpec((1, L, H), lambda s: (s, 0, 0)), pl.BlockSpec((1, H, L), lambda s: (s, 0, 0))],
        out_specs=[pl.BlockSpec((1, L, H), lambda s: (s, 0, 0)), pl.BlockSpec((1, H, L), lambda s: (s, 0, 0))],
        out_shape=[jax.ShapeDtypeStruct((nseq, L, H), F32), jax.ShapeDtypeStruct((nseq, H, L), F32)],
        compiler_params=_cparams(("arbitrary",)),
        name="forget_cumsum",
    )(lf, lft)


FOX_SUB_ROWS = 128


def _fox_kernel(q_ref, g_ref, k_ref, v_ref, cq_ref, ckt_ref, o_ref, *, bq, bk, dh, off, scale):
    hp = pl.program_id(1)
    qi = pl.program_id(2)
    rs = FOX_SUB_ROWS if bq % FOX_SUB_ROWS == 0 else bq
    nsub = bq // rs
    lane = lax.broadcasted_iota(I32, (rs, 2 * dh), 1)
    rowi = lax.broadcasted_iota(I32, (rs, bk), 0)
    kiota = lax.broadcasted_iota(I32, (rs, bk), 1)
    nkb = (off + (qi + 1) * bq + bk - 1) // bk
    head_lane = lax.broadcasted_iota(I32, (rs, cq_ref.shape[2]), 1)
    qh, cq, qpos = [], [], []
    for sb in range(nsub):
        q2 = q_ref[sb * rs:(sb + 1) * rs, :] * scale
        cq_all = cq_ref[0, sb * rs:(sb + 1) * rs, :]
        qpos.append(off + qi * bq + sb * rs + rowi)
        for hh in range(2):
            sel = (lane >= dh) if hh else (lane < dh)
            qh.append(jnp.where(sel, q2, 0.0).astype(BF16))
            cq.append(jnp.sum(jnp.where(head_lane == 2 * hp + hh, cq_all, 0.0), axis=1, keepdims=True))

    def body(kb, carry):
        ks = pl.multiple_of(kb * bk, bk)
        kblk = k_ref[0, pl.ds(ks, bk), :]
        vblk = v_ref[0, pl.ds(ks, bk), :]
        cks = [ckt_ref[0, 2 * hp + hh, :, pl.ds(ks, bk)] for hh in range(2)]
        nch = 2 * nsub
        scores = [lax.dot_general(qh[ch], kblk, NT, preferred_element_type=F32) for ch in range(nch)]
        stats, probs = [], []
        for ch in range(nch):
            m, l, _ = carry[ch]
            causal = kiota + ks <= qpos[ch // 2]
            s = jnp.where(causal, scores[ch] + cq[ch] - cks[ch % 2], -jnp.inf)
            m_new = jnp.maximum(m, jnp.max(s, axis=1, keepdims=True))
            alpha = jnp.exp(m - m_new)
            pexp = jnp.exp(s - m_new)
            stats.append((m_new, alpha * l + jnp.sum(pexp, axis=1, keepdims=True), alpha))
            probs.append(pexp.astype(BF16))
        new = []
        for ch in range(nch):
            m_new, l_new, alpha = stats[ch]
            acc = alpha * carry[ch][2] + jnp.dot(probs[ch], vblk, preferred_element_type=F32)
            new.append((m_new, l_new, acc))
        return tuple(new)

    init = (jnp.full((rs, 1), -jnp.inf, F32), jnp.zeros((rs, 1), F32), jnp.zeros((rs, 2 * dh), F32))
    fin = lax.fori_loop(0, nkb, body, (init,) * (2 * nsub))
    for sb in range(nsub):
        (_, l0, a0), (_, l1, a1) = fin[2 * sb], fin[2 * sb + 1]
        o = jnp.where(lane < dh, a0 / l0, a1 / l1)
        o_ref[sb * rs:(sb + 1) * rs, :] = o * jax.nn.sigmoid(g_ref[sb * rs:(sb + 1) * rs, :])


def fox_attention(qg, k16, v16, ck, ckt, tok0, nseq, L, Lk, dh, bq, bk):
    aw = k16.shape[2]
    Lkp = k16.shape[1]
    H = ck.shape[2]
    off = Lk - L
    nq = L // bq
    tb0 = tok0 // bq
    npair = aw // (2 * dh)
    kern = functools.partial(_fox_kernel, bq=bq, bk=bk, dh=dh, off=off, scale=dh ** -0.5)
    return pl.pallas_call(
        kern,
        grid=(nseq, npair, nq),
        in_specs=[pl.BlockSpec((bq, 2 * dh), lambda s, h, i: (tb0 + s * nq + i, h)),
                  pl.BlockSpec((bq, 2 * dh), lambda s, h, i: (tb0 + s * nq + i, npair + h)),
                  pl.BlockSpec((1, Lkp, 2 * dh), lambda s, h, i: (s, 0, h)),
                  pl.BlockSpec((1, Lkp, 2 * dh), lambda s, h, i: (s, 0, h)),
                  pl.BlockSpec((1, bq, H), lambda s, h, i: (s, off // bq + i, 0)),
                  pl.BlockSpec((1, H, 1, Lkp), lambda s, h, i: (s, 0, 0, 0))],
        out_specs=pl.BlockSpec((bq, 2 * dh), lambda s, h, i: (s * nq + i, h)),
        out_shape=jax.ShapeDtypeStruct((nseq * L, aw), F32),
        compiler_params=_cparams(("arbitrary", "arbitrary", "arbitrary")),
        name="fox_attention",
    )(qg, qg, k16, v16, ck, ckt.reshape(nseq, H, 1, Lkp))


def _pad_lanes(v, n=LANES):
    v = v.reshape(1, -1).astype(F32)
    return jnp.pad(v, ((0, 0), (0, n - v.shape[1])))


def _round_up(n, m):
    return (n + m - 1) // m * m


def kernel(x_prompt, x_sample, state_ssm, state_conv, cache_k, cache_v, cache_logf, a_w_in, a_conv_w, a_conv_b, a_dt_bias, a_A_log, a_D, a_norm_w, a_w_out, kv_w, kv_b_f, b_w_qg, b_w_o, peer_w_q, peer_subkeys, peer_u, peer_v, ln_g, ln_b):
    Bp, S, D = x_prompt.shape
    Bs, Ss, _ = x_sample.shape
    depth = ln_g.shape[0]
    n_a = a_w_in.shape[0]
    H = a_dt_bias.shape[1]
    DI = a_w_out.shape[1]
    P = DI // H
    N = state_ssm.shape[-1]
    C = a_conv_w.shape[2]
    G = (C - DI) // (2 * N)
    past = cache_k.shape[1]
    AH, dh = cache_k.shape[2], cache_k.shape[3]
    AW = AH * dh
    alpha = (2.0 * depth) ** 0.25
    Tp, Ts = Bp * S, Bs * Ss
    T = Tp + Ts
    nseq = Bp + Bs
    assert S % CHUNK == 0 and Ss % CHUNK == 0 and H <= LANES and D % 256 == 0 and LANES % P == 0
    per = LANES // P
    assert (H // G) % per == 0

    seq_np = np.concatenate([np.repeat(np.arange(Bp), S // CHUNK), Bp + np.repeat(np.arange(Bs), Ss // CHUNK)])
    first_np = np.concatenate([[1], (seq_np[1:] != seq_np[:-1]).astype(np.int64)])
    seq = jnp.asarray(seq_np, I32)
    first = jnp.asarray(first_np, I32)

    x = jnp.concatenate([x_prompt.reshape(Tp, D), x_sample.reshape(Ts, D)], axis=0)
    rows_per_expert = D // 256
    new_conv, new_ssm = [], []
    k_f32 = v_f32 = lf = None

    for i in range(depth):
        g1, b1 = ln_g[i, 0].reshape(1, D), ln_b[i, 0].reshape(1, D)
        g2, b2 = ln_g[i, 1].reshape(1, D), ln_b[i, 1].reshape(1, D)
        if i < n_a:
            w_in = jnp.pad(a_w_in[i].astype(BF16), ((0, 0), (0, LANES - H)))
            z, xbc_raw, dt_raw = matmul(x, w_in, (DI, C, LANES))
            conv0 = jnp.concatenate([jnp.zeros((Bp,) + state_conv.shape[2:], F32), state_conv[i]], axis=0)
            ssm0 = jnp.concatenate([jnp.zeros((Bp, H, P, N), F32), state_ssm[i]], axis=0)
            ssm0_t = ssm0.reshape(nseq, H // per, per, P, N).transpose(0, 1, 4, 2, 3).reshape(nseq, H // per, N, LANES)
            y, ssm_new, conv_new = ssd_scan(xbc_raw, conv0, a_conv_w[i], a_conv_b[i].reshape(1, C), dt_raw,
                                            _pad_lanes(a_dt_bias[i]), _pad_lanes(a_A_log[i]),
                                            jnp.repeat(a_D[i].astype(F32), P).reshape(1, DI), ssm0_t, first, seq,
                                            H, P, N, G)
            new_conv.append(conv_new)
            new_ssm.append(ssm_new.reshape(nseq, H // per, N, per, P).transpose(0, 1, 3, 4, 2).reshape(nseq, H, P, N))
            x = mamba_out(y, z, x, a_norm_w[i].reshape(1, DI), a_w_out[i].astype(BF16), g1, b1, G, alpha)
        else:
            j = i - n_a
            (qg,) = matmul(x, b_w_qg[j].astype(BF16), (2 * AW,))
            o_p = fox_attention(qg, kp16, vp16, ck_p, ckt_p, 0, Bp, S, S, dh, bq_p, bk_p)
            o_s = fox_attention(qg, ks16, vs16, ck_s, ckt_s, Tp, Bs, Ss, past + Ss, dh, bq_s, bk_s)
            x = matmul_res_ln(jnp.concatenate([o_p, o_s], axis=0), b_w_o[j].astype(BF16), x, g1, b1, alpha)

        idx, gates = peer_topk(x, peer_w_q[i].astype(BF16), peer_subkeys[i].astype(BF16), rows_per_expert)
        idx_t = idx.T
        act = peer_u_phase(idx_t, x, gates, pack_table(peer_u[i]), rows_per_expert)
        x = peer_v_phase(idx_t, act, x, pack_table(peer_v[i]), g2, b2, rows_per_expert, alpha)

        if i == n_a - 1:
            wkv = kv_w[:, :2 * AW].astype(BF16)
            wf = kv_w[:, 2 * AW:].astype(BF16)
            k_f32, v_f32, k16, v16, lf, lft = shared_kv(x, wkv, wf, wf.T, kv_b_f.reshape(1, AH), kv_b_f.reshape(AH, 1))
            bq_p = _pick(S, 2 * FOX_SUB_ROWS)
            bk_p = _pick(S, 512)
            kp16 = k16[:Tp].reshape(Bp, S, AW)
            vp16 = v16[:Tp].reshape(Bp, S, AW)
            Sp = _round_up(S, LANES)
            lf_p = jnp.pad(lf[:Tp].reshape(Bp, S, AH), ((0, 0), (0, Sp - S), (0, 0)))
            lft_p = jnp.pad(jnp.swapaxes(lft[:, :Tp].reshape(AH, Bp, S), 0, 1), ((0, 0), (0, 0), (0, Sp - S)))
            ck_p, ckt_p = forget_cumsum(lf_p, lft_p)
            Lk = past + Ss
            bq_s = _pick(Ss, 128)
            Lkp = _round_up(Lk, LANES)
            bk_s = max(b for b in range(LANES, 4 * LANES + 1, LANES) if Lkp % b == 0)
            padk = ((0, 0), (0, Lkp - Lk), (0, 0))
            ks16 = jnp.pad(jnp.concatenate([cache_k.reshape(Bs, past, AW).astype(BF16), k16[Tp:].reshape(Bs, Ss, AW)], axis=1), padk)
            vs16 = jnp.pad(jnp.concatenate([cache_v.reshape(Bs, past, AW).astype(BF16), v16[Tp:].reshape(Bs, Ss, AW)], axis=1), padk)
            lf_s = jnp.pad(jnp.concatenate([cache_logf.astype(F32), lf[Tp:].reshape(Bs, Ss, AH)], axis=1), padk)
            lft_new = jnp.swapaxes(lft[:, Tp:].reshape(AH, Bs, Ss), 0, 1)
            lft_s = jnp.pad(jnp.concatenate([jnp.swapaxes(cache_logf.astype(F32), 1, 2), lft_new], axis=2),
                            ((0, 0), (0, 0), (0, Lkp - Lk)))
            ck_s, ckt_s = forget_cumsum(lf_s, lft_s)

    conv_all = jnp.stack(new_conv)
    ssm_all = jnp.stack(new_ssm)
    return (x[:Tp].reshape(Bp, S, D), x[Tp:].reshape(Bs, Ss, D),
            ssm_all[:, :Bp], conv_all[:, :Bp],
            k_f32[:Tp].reshape(Bp, S, AH, dh), v_f32[:Tp].reshape(Bp, S, AH, dh), lf[:Tp].reshape(Bp, S, AH),
            ssm_all[:, Bp:], conv_all[:, Bp:],
            k_f32[Tp:].reshape(Bs, Ss, AH, dh), v_f32[Tp:].reshape(Bs, Ss, AH, dh), lf[Tp:].reshape(Bs, Ss, AH))
```
